```python
import jax
import jax.numpy as jnp
from jax import lax
import numpy as np

D_MODEL = 4096
BATCH = 16
SEQ = 256
DEPTH = 2
DEC_BATCH = 8
DEC_SEQ = 4096
PAST_LEN = 512

GRID_W = 64
HEAD_DIM = 128
N_GROUP_HEADS = D_MODEL // (4 * HEAD_DIM)
NA_HEADS = N_GROUP_HEADS
NA_WIN_H = 8
NA_WIN_W = 16
NA_QB_W = 16
NA_KB_W = NA_QB_W + NA_WIN_W
SWA_HEADS = N_GROUP_HEADS
SWA_KV_HEADS = SWA_HEADS // 4
SWA_WINDOW = 128
SWA_BLOCK = 128
CTX_QBLOCK = 128
FNET_GROUPS = N_GROUP_HEADS
FNET_GROUP_DIM = HEAD_DIM
NA_WIDTH = NA_HEADS * HEAD_DIM
SWA_WIDTH = SWA_HEADS * HEAD_DIM
SWA_KV_WIDTH = SWA_KV_HEADS * HEAD_DIM
FNET_WIDTH = FNET_GROUPS * FNET_GROUP_DIM
MIX_WIDTH = NA_WIDTH + SWA_WIDTH + FNET_WIDTH
IN_SPLITS = (NA_WIDTH, 2 * NA_WIDTH, 3 * NA_WIDTH, 3 * NA_WIDTH + SWA_WIDTH,
             3 * NA_WIDTH + SWA_WIDTH + SWA_KV_WIDTH, 3 * NA_WIDTH + SWA_WIDTH + 2 * SWA_KV_WIDTH)
IN_COLS = IN_SPLITS[-1] + FNET_WIDTH
D_FF_RAW = (8 * D_MODEL) // 3
D_FF = 256 * ((D_FF_RAW + 255) // 256)
CONV_W = 3
ROPE_BASE = 10000.0
EPS = 1e-6
NEG_INF = -1e30

kernel_name = 'hybrid_natten_swa_fnet_dit_step'


def rms_norm(x, g):
    xf = x.astype(jnp.float32)
    y = xf * lax.rsqrt(jnp.mean(xf * xf, axis=-1, keepdims=True) + EPS)
    return (y * g.astype(jnp.float32)).astype(x.dtype)


def ada_modulation(cond, w, b):
    m = jax.nn.silu(cond) @ w + b
    return jnp.split(m[:, None, :], 6, axis=-1)


def modulate(x, g, shift, scale):
    return rms_norm(x, g) * (1 + scale) + shift


def gated_residual(x, y, g_post, gate):
    return x + gate * rms_norm(y, g_post)


def project(h, w_in):
    b, s, _ = h.shape
    na_q, na_k, na_v, sw_q, sw_k, sw_v, fu = jnp.split(h @ w_in, IN_SPLITS, axis=-1)
    r = lambda t, nh: t.reshape(b, s, nh, HEAD_DIM)
    return (r(na_q, NA_HEADS), r(na_k, NA_HEADS), r(na_v, NA_HEADS),
            r(sw_q, SWA_HEADS), r(sw_k, SWA_KV_HEADS), r(sw_v, SWA_KV_HEADS), fu)


def axial_rope_tables(s):
    t = jnp.arange(s)
    pos = jnp.stack([t // GRID_W, t % GRID_W], axis=-1).astype(jnp.float32)
    nf = HEAD_DIM // 4
    inv_freq = ROPE_BASE ** (-jnp.arange(nf, dtype=jnp.float32) / nf)
    ang = pos[:, :, None] * inv_freq
    return jnp.cos(ang), jnp.sin(ang)


def apply_axial_rope(x, cos, sin):
    b, s, h, d = x.shape
    xa = x.astype(jnp.float32).reshape(b, s, h, 2, 2, d // 4)
    x1, x2 = xa[..., 0, :], xa[..., 1, :]
    cs, sn = cos[None, :, None], sin[None, :, None]
    out = jnp.stack([x1 * cs - x2 * sn, x2 * cs + x1 * sn], axis=-2)
    return out.reshape(b, s, h, d).astype(x.dtype)


def context_attention(q, k, v, sink):
    b, s, hq, hd = q.shape
    hkv = k.shape[2]
    g = hq // hkv
    nb = s // CTX_QBLOCK
    scale = hd ** -0.5
    qb = q.reshape(b, nb, CTX_QBLOCK, hkv, g, hd).transpose(1, 0, 2, 3, 4, 5)

    def one_block(qj):
        l = jnp.einsum('bqkgd,bskd->bkgqs', qj, k, preferred_element_type=jnp.float32) * scale
        if sink is not None:
            sl = jnp.broadcast_to(sink.astype(jnp.float32).reshape(1, hkv, g, 1, 1), l.shape[:-1] + (1,))
            l = jnp.concatenate([l, sl], axis=-1)
        p = jax.nn.softmax(l, axis=-1)[..., :s].astype(v.dtype)
        return jnp.einsum('bkgqs,bskd->bqkgd', p, v).reshape(b, CTX_QBLOCK, hq * hd)

    out = lax.map(one_block, qb)
    return out.transpose(1, 0, 2, 3).reshape(b, s, hq * hd)


def neighbourhood_attention(q, k, v, ck, cv, rel_bias):
    b, s, h, hd = q.shape
    rows = s // GRID_W
    kh = min(NA_WIN_H, rows)
    ncb = GRID_W // NA_QB_W
    scale = hd ** -0.5
    qcol = np.arange(GRID_W).reshape(ncb, NA_QB_W)
    band0 = np.clip(np.arange(ncb) * NA_QB_W - NA_WIN_W // 2, 0, GRID_W - NA_KB_W)
    kcol = band0[:, None] + np.arange(NA_KB_W)
    cstart = np.clip(qcol - NA_WIN_W // 2, 0, GRID_W - NA_WIN_W)
    col_ok = jnp.asarray((kcol[:, None, :] >= cstart[:, :, None]) &
                         (kcol[:, None, :] < cstart[:, :, None] + NA_WIN_W))
    dcol = np.clip(kcol[:, None, :] - qcol[:, :, None] + NA_WIN_W - 1, 0, 2 * NA_WIN_W - 2)
    bias_col = rel_bias.astype(jnp.float32)[:, :, dcol]
    kg = k.reshape(b, rows, GRID_W, h, hd)
    vg = v.reshape(b, rows, GRID_W, h, hd)
    qg = q.reshape(b, rows, ncb, NA_QB_W, h, hd).transpose(1, 0, 2, 3, 4, 5)
    nw = kh * NA_KB_W

    def one_row(args):
        r, qr = args
        rs = jnp.clip(r - NA_WIN_H // 2, 0, rows - kh)
        kb = lax.dynamic_slice_in_dim(kg, rs, kh, axis=1)[:, :, kcol]
        vb = lax.dynamic_slice_in_dim(vg, rs, kh, axis=1)[:, :, kcol]
        drow = rs + jnp.arange(kh) - r + NA_WIN_H - 1
        bias = jnp.transpose(bias_col[:, drow], (0, 2, 3, 1, 4))
        lw = jnp.einsum('bjqhd,bijmhd->bhjqim', qr, kb, preferred_element_type=jnp.float32) * scale + bias[None]
        lw = jnp.where(col_ok[None, None, :, :, None, :], lw, NEG_INF).reshape(b, h, ncb, NA_QB_W, nw)
        lc = jnp.einsum('bjqhd,bshd->bhjqs', qr, ck, preferred_element_type=jnp.float32) * scale
        p = jax.nn.softmax(jnp.concatenate([lw, lc], axis=-1), axis=-1).astype(v.dtype)
        pw = p[..., :nw].reshape(b, h, ncb, NA_QB_W, kh, NA_KB_W)
        return (jnp.einsum('bhjqim,bijmhd->bjqhd', pw, vb) +
                jnp.einsum('bhjqs,bshd->bjqhd', p[..., nw:], cv))

    out = lax.map(one_row, (jnp.arange(rows), qg))
    return out.transpose(1, 0, 2, 3, 4, 5).reshape(b, s, h * hd)


def window_attention(q, k, v, ck, cv, sink):
    b, s, hq, hd = q.shape
    hkv = k.shape[2]
    g = hq // hkv
    nb = s // SWA_BLOCK
    nw = 3 * SWA_BLOCK
    scale = hd ** -0.5
    pad = ((0, 0), (SWA_BLOCK, SWA_BLOCK), (0, 0), (0, 0))
    kp = jnp.pad(k, pad)
    vp = jnp.pad(v, pad)
    qb = q.reshape(b, nb, SWA_BLOCK, hkv, g, hd).transpose(1, 0, 2, 3, 4, 5)
    sink_l = sink.astype(jnp.float32).reshape(1, hkv, g, 1, 1)

    def one_block(args):
        j, qj = args
        start = j * SWA_BLOCK
        kj = lax.dynamic_slice_in_dim(kp, start, nw, axis=1)
        vj = lax.dynamic_slice_in_dim(vp, start, nw, axis=1)
        qpos = start + jnp.arange(SWA_BLOCK)
        kpos = start - SWA_BLOCK + jnp.arange(nw)
        valid = ((jnp.abs(qpos[:, None] - kpos[None, :]) <= SWA_WINDOW) &
                 (kpos >= 0)[None, :] & (kpos < s)[None, :])
        lw = jnp.einsum('bqkgd,bskd->bkgqs', qj, kj, preferred_element_type=jnp.float32) * scale
        lw = jnp.where(valid, lw, NEG_INF)
        lc = jnp.einsum('bqkgd,bskd->bkgqs', qj, ck, preferred_element_type=jnp.float32) * scale
        ls = jnp.broadcast_to(sink_l, lw.shape[:-1] + (1,))
        p = jax.nn.softmax(jnp.concatenate([lw, lc, ls], axis=-1), axis=-1).astype(v.dtype)
        o = (jnp.einsum('bkgqs,bskd->bqkgd', p[..., :nw], vj) +
             jnp.einsum('bkgqs,bskd->bqkgd', p[..., nw:nw + ck.shape[1]], cv))
        return o.reshape(b, SWA_BLOCK, hq * hd)

    out = lax.map(one_block, (jnp.arange(nb), qb))
    return out.transpose(1, 0, 2, 3).reshape(b, s, hq * hd)


def fourier_mix(u, w):
    b, s, _ = u.shape
    ug = u.astype(jnp.float32).reshape(b, s, FNET_GROUPS, FNET_GROUP_DIM)
    f = jnp.fft.fft2(ug, axes=(1, 3), norm='ortho').real.astype(u.dtype)
    return jnp.einsum('bsgc,gce->bsge', f, w).reshape(b, s, FNET_WIDTH)


def merge_heads(o_na, o_sw, o_fn, mix_norm, w_out):
    o = jnp.concatenate([rms_norm(o_na, mix_norm[:NA_WIDTH]),
                         rms_norm(o_sw, mix_norm[NA_WIDTH:NA_WIDTH + SWA_WIDTH]),
                         rms_norm(o_fn, mix_norm[NA_WIDTH + SWA_WIDTH:])], axis=-1)
    return o @ w_out


def context_mixer(h, lp):
    na_q, na_k, na_v, sw_q, sw_k, sw_v, fu = project(h, lp['w_in'])
    o_na = context_attention(na_q, na_k, na_v, None)
    o_sw = context_attention(sw_q, sw_k, sw_v, lp['swa_sink'])
    o_fn = fourier_mix(fu, lp['fnet_w'])
    return merge_heads(o_na, o_sw, o_fn, lp['mix_norm'], lp['w_out']), (na_k, na_v, sw_k, sw_v)


def latent_mixer(h, ck_na, cv_na, ck_sw, cv_sw, lp):
    na_q, na_k, na_v, sw_q, sw_k, sw_v, fu = project(h, lp['w_in'])
    cos, sin = axial_rope_tables(h.shape[1])
    sw_q = apply_axial_rope(sw_q, cos, sin)
    sw_k = apply_axial_rope(sw_k, cos, sin)
    o_na = neighbourhood_attention(na_q, na_k, na_v, ck_na, cv_na, lp['na_bias'])
    o_sw = window_attention(sw_q, sw_k, sw_v, ck_sw, cv_sw, lp['swa_sink'])
    o_fn = fourier_mix(fu, lp['fnet_w'])
    return merge_heads(o_na, o_sw, o_fn, lp['mix_norm'], lp['w_out'])


def conv_ffn(h, w_up, conv_w, conv_b, w_down):
    s = h.shape[1]
    u = jnp.pad(h @ w_up, ((0, 0), (CONV_W // 2, CONV_W // 2), (0, 0)))
    u = sum(u[:, i:i + s] * conv_w[i] for i in range(CONV_W)) + conv_b
    gate, val = jnp.split(u, 2, axis=-1)
    return (jax.nn.silu(gate) * val) @ w_down


def ffn_sublayer(x, lp, shift, scale, gate):
    h = modulate(x, lp['norm_ffn_pre'], shift, scale)
    y = conv_ffn(h, lp['ffn_w_up'], lp['ffn_conv_w'], lp['ffn_conv_b'], lp['ffn_w_down'])
    return gated_residual(x, y, lp['norm_ffn_post'], gate)


def setup_inputs(seed: int = 0) -> dict:
    key = jax.random.key(seed)
    ks = jax.random.split(key, 24)
    f32 = jnp.float32

    def nrm(k, shape, scale):
        return jax.random.normal(k, shape, f32) * scale

    def gain(k, shape):
        return 1.0 + nrm(k, shape, 0.05)

    return {
        'x_prompt': nrm(ks[0], (BATCH, SEQ, D_MODEL), 1.0),
        'x_sample': nrm(ks[1], (DEC_BATCH, DEC_SEQ, D_MODEL), 1.0),
        'c': nrm(ks[2], (DEC_BATCH, D_MODEL), 1.0),
        'cache_nat_k': nrm(ks[3], (DEC_BATCH, DEPTH, PAST_LEN, NA_HEADS, HEAD_DIM), 1.0),
        'cache_nat_v': nrm(ks[4], (DEC_BATCH, DEPTH, PAST_LEN, NA_HEADS, HEAD_DIM), 1.0),
        'cache_swa_k': nrm(ks[5], (DEC_BATCH, DEPTH, PAST_LEN, SWA_KV_HEADS, HEAD_DIM), 1.0),
        'cache_swa_v': nrm(ks[6], (DEC_BATCH, DEPTH, PAST_LEN, SWA_KV_HEADS, HEAD_DIM), 1.0),
        'c_ctx': nrm(ks[7], (D_MODEL,), 1.0),
        'ada_w': nrm(ks[8], (DEPTH, D_MODEL, 6 * D_MODEL), D_MODEL ** -0.5),
        'ada_b': nrm(ks[9], (DEPTH, 6 * D_MODEL), 0.01),
        'norm_attn_pre': gain(ks[10], (DEPTH, D_MODEL)),
        'norm_attn_post': gain(ks[11], (DEPTH, D_MODEL)),
        'norm_ffn_pre': gain(ks[12], (DEPTH, D_MODEL)),
        'norm_ffn_post': gain(ks[13], (DEPTH, D_MODEL)),
        'w_in': nrm(ks[14], (DEPTH, D_MODEL, IN_COLS), D_MODEL ** -0.5),
        'na_bias': nrm(ks[15], (DEPTH, NA_HEADS, 2 * NA_WIN_H - 1, 2 * NA_WIN_W - 1), 0.1),
        'swa_sink': nrm(ks[16], (DEPTH, SWA_HEADS), 0.5),
        'fnet_w': nrm(ks[17], (DEPTH, FNET_GROUPS, FNET_GROUP_DIM, FNET_GROUP_DIM), FNET_GROUP_DIM ** -0.5),
        'mix_norm': gain(ks[18], (DEPTH, MIX_WIDTH)),
        'w_out': nrm(ks[19], (DEPTH, MIX_WIDTH, D_MODEL), MIX_WIDTH ** -0.5),
        'ffn_w_up': nrm(ks[20], (DEPTH, D_MODEL, 2 * D_FF), D_MODEL ** -0.5),
        'ffn_conv_w': nrm(ks[21], (DEPTH, CONV_W, 2 * D_FF), CONV_W ** -0.5),
        'ffn_conv_b': nrm(ks[22], (DEPTH, 2 * D_FF), 0.01),
        'ffn_w_down': nrm(ks[23], (DEPTH, D_FF, D_MODEL), D_FF ** -0.5),
    }


def reference(x_prompt, x_sample, c, cache_nat_k, cache_nat_v, cache_swa_k, cache_swa_v, c_ctx,
              ada_w, ada_b, norm_attn_pre, norm_attn_post, norm_ffn_pre, norm_ffn_post,
              w_in, na_bias, swa_sink, fnet_w, mix_norm, w_out,
              ffn_w_up, ffn_conv_w, ffn_conv_b, ffn_w_down):
    def layer_params(l):
        return {'norm_attn_pre': norm_attn_pre[l], 'norm_attn_post': norm_attn_post[l],
                'norm_ffn_pre': norm_ffn_pre[l], 'norm_ffn_post': norm_ffn_post[l],
                'w_in': w_in[l], 'na_bias': na_bias[l], 'swa_sink': swa_sink[l],
                'fnet_w': fnet_w[l], 'mix_norm': mix_norm[l], 'w_out': w_out[l],
                'ffn_w_up': ffn_w_up[l], 'ffn_conv_w': ffn_conv_w[l],
                'ffn_conv_b': ffn_conv_b[l], 'ffn_w_down': ffn_w_down[l]}

    xp = x_prompt
    nat_k, nat_v, swa_k, swa_v = [], [], [], []
    for l in range(DEPTH):
        lp = layer_params(l)
        sh_a, sc_a, g_a, sh_f, sc_f, g_f = ada_modulation(c_ctx[None, :], ada_w[l], ada_b[l])
        h = modulate(xp, lp['norm_attn_pre'], sh_a, sc_a)
        y, (nk, nv, sk, sv) = context_mixer(h, lp)
        xp = gated_residual(xp, y, lp['norm_attn_post'], g_a)
        xp = ffn_sublayer(xp, lp, sh_f, sc_f, g_f)
        nat_k.append(nk)
        nat_v.append(nv)
        swa_k.append(sk)
        swa_v.append(sv)

    xs = x_sample
    for l in range(DEPTH):
        lp = layer_params(l)
        sh_a, sc_a, g_a, sh_f, sc_f, g_f = ada_modulation(c, ada_w[l], ada_b[l])
        h = modulate(xs, lp['norm_attn_pre'], sh_a, sc_a)
        y = latent_mixer(h, cache_nat_k[:, l], cache_nat_v[:, l], cache_swa_k[:, l], cache_swa_v[:, l], lp)
        xs = gated_residual(xs, y, lp['norm_attn_post'], g_a)
        xs = ffn_sublayer(xs, lp, sh_f, sc_f, g_f)

    return (xp, xs, jnp.stack(nat_k, axis=1), jnp.stack(nat_v, axis=1),
            jnp.stack(swa_k, axis=1), jnp.stack(swa_v, axis=1))
```

```python
import functools

import numpy as np
import jax
import jax.numpy as jnp
from jax import lax
from jax.experimental import pallas as pl
from jax.experimental.pallas import tpu as pltpu

GRID_W = 64
HEAD_DIM = 128
NA_WIN_H = 8
NA_WIN_W = 16
NA_TILE_ROWS = 4
NA_KEY_ROWS = NA_TILE_ROWS + NA_WIN_H
SWA_WINDOW = 128
SWA_BLOCK = 128
CONV_W = 3
ROPE_BASE = 10000.0
EPS = 1e-6
NEG_INF = -1e30
ATTN_SCALE = HEAD_DIM ** -0.5
HALO = 16
FF_ALIGN = 512
MIB = 1024 * 1024
BF16 = jnp.bfloat16
F32 = jnp.float32


def _params(sem, vmem_mib):
    return pltpu.CompilerParams(dimension_semantics=sem, vmem_limit_bytes=vmem_mib * MIB)


def _tile(n, want, align=128):
    if n <= want:
        return n
    t = want - want % align
    while t > align and n % t:
        t -= align
    assert t >= align and n % t == 0, (n, want)
    return t


def _batch_of(i, tm, seq, nb):
    return (i * tm) // seq if nb > 1 else 0


def _dot(a, b):
    return jnp.dot(a, b, preferred_element_type=F32)


def _dot_nt(a, b):
    return lax.dot_general(a, b, (((1,), (1,)), ((), ())), preferred_element_type=F32)


def _ada_kernel(c_ref, w_ref, b_ref, o_ref):
    c = c_ref[...]
    s = (c * (1.0 / (1.0 + jnp.exp(-c)))).astype(BF16)
    o_ref[0] = _dot(s, w_ref[0].astype(BF16)) + b_ref[0]


def _ada(cond, ada_w, ada_b):
    depth, d, n = ada_w.shape
    r = cond.shape[0]
    tn = _tile(n, 512)
    return pl.pallas_call(
        _ada_kernel,
        grid=(depth, n // tn),
        in_specs=[pl.BlockSpec((r, d), lambda l, j: (0, 0)),
                  pl.BlockSpec((1, d, tn), lambda l, j: (l, 0, j)),
                  pl.BlockSpec((1, 1, tn), lambda l, j: (l, 0, j))],
        out_specs=pl.BlockSpec((1, r, tn), lambda l, j: (l, 0, j)),
        out_shape=jax.ShapeDtypeStruct((depth, r, n), F32),
        compiler_params=_params(("parallel", "parallel"), 40),
    )(cond, ada_w, ada_b.reshape(depth, 1, n))


def _modulate_kernel(x_ref, g_ref, sh_ref, sc_ref, o_ref):
    x = x_ref[...]
    y = x * lax.rsqrt(jnp.mean(x * x, axis=-1, keepdims=True) + EPS) * g_ref[...]
    o_ref[...] = (y * (1.0 + sc_ref[0]) + sh_ref[0]).astype(o_ref.dtype)


def _modulate(x, g, shift, scale, seq):
    m, d = x.shape
    nb = shift.shape[0]
    tm = _tile(m, 256)
    mod_spec = pl.BlockSpec((1, 1, d), lambda i: (_batch_of(i, tm, seq, nb), 0, 0))
    return pl.pallas_call(
        _modulate_kernel,
        grid=(m // tm,),
        in_specs=[pl.BlockSpec((tm, d), lambda i: (i, 0)),
                  pl.BlockSpec((1, d), lambda i: (0, 0)), mod_spec, mod_spec],
        out_specs=pl.BlockSpec((tm, d), lambda i: (i, 0)),
        out_shape=jax.ShapeDtypeStruct((m, d), BF16),
        compiler_params=_params(("parallel",), 40),
    )(x, g.reshape(1, d), shift, scale)


def _mm_kernel(a_ref, w_ref, o_ref):
    o_ref[...] = _dot(a_ref[...].astype(BF16), w_ref[...]).astype(o_ref.dtype)


def _mm(a, w, out_dtype, tm=1024, tn=512):
    m = a.shape[0]
    k, n = w.shape
    tm, tn = _tile(m, tm), _tile(n, tn)
    return pl.pallas_call(
        _mm_kernel,
        grid=(m // tm, n // tn),
        in_specs=[pl.BlockSpec((tm, k), lambda i, j: (i, 0)),
                  pl.BlockSpec((k, tn), lambda i, j: (0, j))],
        out_specs=pl.BlockSpec((tm, tn), lambda i, j: (i, j)),
        out_shape=jax.ShapeDtypeStruct((m, n), out_dtype),
        compiler_params=_params(("parallel", "arbitrary"), 48),
    )(a, w)


def _ffn_up_kernel(prev_ref, a_ref, next_ref, wg_ref, wv_ref, cwg_ref, cwv_ref, cbg_ref, cbv_ref,
                   o_ref, lhs_ref, *, tm, seq):
    i, j = pl.program_id(0), pl.program_id(1)

    @pl.when(j == 0)
    def _():
        lhs_ref[0:HALO, :] = prev_ref[...]
        lhs_ref[HALO:HALO + tm, :] = a_ref[...]
        lhs_ref[HALO + tm:, :] = next_ref[...]

    pos = (i * tm + lax.broadcasted_iota(jnp.int32, (tm, 1), 0)) % seq
    has_prev = pos != 0
    has_next = pos != seq - 1
    rows = tm + 2 * HALO

    def conv(w_ref, cw_ref, cb_ref):
        u = _dot(lhs_ref[...], w_ref[...])
        up = pltpu.roll(u, 1, 0)[HALO:HALO + tm]
        un = pltpu.roll(u, rows - 1, 0)[HALO:HALO + tm]
        uc = u[HALO:HALO + tm]
        return (jnp.where(has_prev, up, 0.0) * cw_ref[0:1, :] + uc * cw_ref[1:2, :]
                + jnp.where(has_next, un, 0.0) * cw_ref[2:3, :] + cb_ref[...])

    gate = conv(wg_ref, cwg_ref, cbg_ref)
    val = conv(wv_ref, cwv_ref, cbv_ref)
    o_ref[...] = (gate * (1.0 / (1.0 + jnp.exp(-gate))) * val).astype(o_ref.dtype)


def _ffn_up(h, w_up, conv_w, conv_b, seq, tm=1024, tn=512):
    m, d = h.shape
    f = w_up.shape[1] // 2
    tm, tn = _tile(m, tm), _tile(f, tn)
    nj = f // tn
    hb = tm // HALO
    last = m // HALO - 1
    kern = functools.partial(_ffn_up_kernel, tm=tm, seq=seq)
    return pl.pallas_call(
        kern,
        grid=(m // tm, nj),
        in_specs=[pl.BlockSpec((HALO, d), lambda i, j: (jnp.maximum(i * hb - 1, 0), 0)),
                  pl.BlockSpec((tm, d), lambda i, j: (i, 0), pipeline_mode=pl.Buffered(1)),
                  pl.BlockSpec((HALO, d), lambda i, j: (jnp.minimum((i + 1) * hb, last), 0)),
                  pl.BlockSpec((d, tn), lambda i, j: (0, j)),
                  pl.BlockSpec((d, tn), lambda i, j: (0, j + nj)),
                  pl.BlockSpec((CONV_W, tn), lambda i, j: (0, j)),
                  pl.BlockSpec((CONV_W, tn), lambda i, j: (0, j + nj)),
                  pl.BlockSpec((1, tn), lambda i, j: (0, j)),
                  pl.BlockSpec((1, tn), lambda i, j: (0, j + nj))],
        out_specs=pl.BlockSpec((tm, tn), lambda i, j: (i, j)),
        out_shape=jax.ShapeDtypeStruct((m, f), BF16),
        scratch_shapes=[pltpu.VMEM((tm + 2 * HALO, d), BF16)],
        compiler_params=_params(("parallel", "arbitrary"), 56),
    )(h, h, h, w_up, w_up, conv_w, conv_w, conv_b, conv_b)


def _mm_res_kernel(a_ref, w_ref, x_ref, gate_ref, g_ref, o_ref, acc_ref, rs_ref, *, kt, tc):
    k = pl.program_id(1)

    @pl.when(k == 0)
    def _():
        acc_ref[...] = _dot(a_ref[...], w_ref[...])

    @pl.when(jnp.logical_and(k > 0, k < kt))
    def _():
        acc_ref[...] += _dot(a_ref[...], w_ref[...])

    @pl.when(k == kt)
    def _():
        y = acc_ref[...]
        rs_ref[...] = lax.rsqrt(jnp.mean(y * y, axis=-1, keepdims=True) + EPS)

    for c in range(acc_ref.shape[1] // tc):
        @pl.when(k == kt + c)
        def _(c=c):
            y = acc_ref[:, c * tc:(c + 1) * tc] * rs_ref[...] * g_ref[...]
            o_ref[...] = x_ref[...] + gate_ref[0] * y


def _mm_res(a, w, x, gate, g_post, seq, tm=1024, tk=512, tc=1024):
    m, kdim = a.shape
    d = w.shape[1]
    nb = gate.shape[0]
    tm, tk, tc = _tile(m, tm), _tile(kdim, tk), _tile(d, tc)
    kt, nc = kdim // tk, d // tc
    kk = lambda k: jnp.minimum(k, kt - 1)
    cc = lambda k: jnp.clip(k - kt, 0, nc - 1)
    kern = functools.partial(_mm_res_kernel, kt=kt, tc=tc)
    return pl.pallas_call(
        kern,
        grid=(m // tm, kt + nc),
        in_specs=[pl.BlockSpec((tm, tk), lambda i, k: (i, kk(k))),
                  pl.BlockSpec((tk, d), lambda i, k: (kk(k), 0)),
                  pl.BlockSpec((tm, tc), lambda i, k: (i, cc(k))),
                  pl.BlockSpec((1, 1, tc), lambda i, k: (_batch_of(i, tm, seq, nb), 0, cc(k))),
                  pl.BlockSpec((1, tc), lambda i, k: (0, cc(k)))],
        out_specs=pl.BlockSpec((tm, tc), lambda i, k: (i, cc(k))),
        out_shape=jax.ShapeDtypeStruct((m, d), F32),
        scratch_shapes=[pltpu.VMEM((tm, d), F32), pltpu.VMEM((tm, 1), F32)],
        compiler_params=_params(("parallel", "arbitrary"), 56),
    )(a, w, x, gate, g_post.reshape(1, d))


def _group_norm_kernel(a0_ref, a1_ref, a2_ref, g_ref, o_ref):
    w = a0_ref.shape[1]
    for n, ref in enumerate((a0_ref, a1_ref, a2_ref)):
        a = ref[...].astype(F32)
        y = a * lax.rsqrt(jnp.mean(a * a, axis=-1, keepdims=True) + EPS) * g_ref[:, n * w:(n + 1) * w]
        o_ref[:, n * w:(n + 1) * w] = y.astype(o_ref.dtype)


def _group_norm(parts, gain):
    m = parts[0][0].shape[0]
    w = gain.shape[0] // 3
    tm = _tile(m, 512)
    specs = [pl.BlockSpec((tm, w), lambda i, cb=cb: (i, cb)) for _, cb in parts]
    return pl.pallas_call(
        _group_norm_kernel,
        grid=(m // tm,),
        in_specs=specs + [pl.BlockSpec((1, 3 * w), lambda i: (0, 0))],
        out_specs=pl.BlockSpec((tm, 3 * w), lambda i: (i, 0)),
        out_shape=jax.ShapeDtypeStruct((m, 3 * w), BF16),
        compiler_params=_params(("parallel",), 32),
    )(*[arr for arr, _ in parts], gain.reshape(1, 3 * w))


def _ctx_attn_kernel(sink_ref, q_ref, k_ref, v_ref, o_ref):
    sink = sink_ref[pl.program_id(1)]
    q = q_ref[0].astype(BF16)
    s = _dot_nt(q, k_ref[0].astype(BF16)) * ATTN_SCALE
    mx = jnp.maximum(jnp.max(s, axis=-1, keepdims=True), sink)
    p = jnp.exp(s - mx)
    den = jnp.sum(p, axis=-1, keepdims=True) + jnp.exp(sink - mx)
    o = _dot(p.astype(BF16), v_ref[0].astype(BF16))
    o_ref[0] = (o / den).astype(o_ref.dtype)


def _ctx_attn(qkv, sink, lay):
    b, s, _ = qkv.shape
    h, rep = lay["heads"], lay["gqa"]
    grp = lambda hh: hh >= h
    qcol = lambda hh: jnp.where(grp(hh), lay["sw_q"] + hh - h, lay["na_q"] + hh)
    kcol = lambda hh: jnp.where(grp(hh), lay["sw_k"] + (hh - h) // rep, lay["na_k"] + hh)
    vcol = lambda hh: jnp.where(grp(hh), lay["sw_v"] + (hh - h) // rep, lay["na_v"] + hh)
    blk = lambda col: pl.BlockSpec((1, s, HEAD_DIM), lambda bi, hh: (bi, 0, col(hh)))
    return pl.pallas_call(
        _ctx_attn_kernel,
        grid=(b, 2 * h),
        in_specs=[pl.BlockSpec(memory_space=pltpu.SMEM), blk(qcol), blk(kcol), blk(vcol)],
        out_specs=pl.BlockSpec((1, s, HEAD_DIM), lambda bi, hh: (bi, 0, hh)),
        out_shape=jax.ShapeDtypeStruct((b, s, 2 * h * HEAD_DIM), BF16),
        compiler_params=_params(("parallel", "parallel"), 32),
    )(sink, qkv, qkv, qkv)


def _na_attn_kernel(q_ref, k_ref, v_ref, ck_ref, cv_ref, bias_ref, o_ref, *, rows):
    t = pl.program_id(2)
    nkeys = NA_KEY_ROWS * GRID_W
    row0 = jnp.clip(NA_TILE_ROWS * t - NA_WIN_H // 2, 0, rows - NA_KEY_ROWS)
    start = pl.multiple_of(row0 * GRID_W, GRID_W)
    q = q_ref[0]
    s_w = _dot_nt(q, k_ref[0, pl.ds(start, nkeys), :]) * ATTN_SCALE + bias_ref[0, 0]
    s_c = _dot_nt(q, ck_ref[0, 0].astype(BF16)) * ATTN_SCALE
    mx = jnp.maximum(jnp.max(s_w, axis=-1, keepdims=True), jnp.max(s_c, axis=-1, keepdims=True))
    p_w = jnp.exp(s_w - mx)
    p_c = jnp.exp(s_c - mx)
    den = jnp.sum(p_w, axis=-1, keepdims=True) + jnp.sum(p_c, axis=-1, keepdims=True)
    o = (_dot(p_w.astype(BF16), v_ref[0, pl.ds(start, nkeys), :])
         + _dot(p_c.astype(BF16), cv_ref[0, 0].astype(BF16)))
    o_ref[0] = (o / den).astype(o_ref.dtype)


def _na_bias_table(rel_bias, rows):
    tq = NA_TILE_ROWS
    kinds = ((0, 0), (tq, 0), (rows - tq, rows - NA_KEY_ROWS))
    ql, qc = np.divmod(np.arange(tq * GRID_W), GRID_W)
    kl, kc = np.divmod(np.arange(NA_KEY_ROWS * GRID_W), GRID_W)
    idx, ok = [], []
    for r0, k0 in kinds:
        r = (r0 + ql)[:, None]
        krow = (k0 + kl)[None, :]
        rs = np.clip(r - NA_WIN_H // 2, 0, rows - NA_WIN_H)
        row_ok = (krow >= rs) & (krow < rs + NA_WIN_H)
        cs = np.clip(qc - NA_WIN_W // 2, 0, GRID_W - NA_WIN_W)[:, None]
        col_ok = (kc[None, :] >= cs) & (kc[None, :] < cs + NA_WIN_W)
        drow = np.clip(krow - r + NA_WIN_H - 1, 0, 2 * NA_WIN_H - 2)
        dcol = np.clip(kc[None, :] - qc[:, None] + NA_WIN_W - 1, 0, 2 * NA_WIN_W - 2)
        idx.append(drow * (2 * NA_WIN_W - 1) + dcol)
        ok.append(row_ok & col_ok)
    idx, ok = np.stack(idx), np.stack(ok)
    flat = rel_bias.astype(F32).reshape(rel_bias.shape[0], -1)
    return jnp.where(jnp.asarray(ok)[None], flat[:, idx], NEG_INF)


def _na_attn(qkv, ck, cv, layer, bias, lay):
    b, s, _ = qkv.shape
    h = lay["heads"]
    rows = s // GRID_W
    nt = rows // NA_TILE_ROWS
    assert rows % NA_TILE_ROWS == 0 and rows >= NA_KEY_ROWS and nt >= 2
    tq, tk = NA_TILE_ROWS * GRID_W, NA_KEY_ROWS * GRID_W
    p = ck.shape[2]
    kind = lambda t: jnp.where(t == 0, 0, jnp.where(t == nt - 1, 2, 1))
    full = lambda col: pl.BlockSpec((1, s, HEAD_DIM), lambda bi, hh, t: (bi, 0, col + hh))
    cache = pl.BlockSpec((1, 1, p, HEAD_DIM), lambda bi, hh, t: (bi, layer, 0, hh))
    return pl.pallas_call(
        functools.partial(_na_attn_kernel, rows=rows),
        grid=(b, h, nt),
        in_specs=[pl.BlockSpec((1, tq, HEAD_DIM), lambda bi, hh, t: (bi, t, lay["na_q"] + hh)),
                  full(lay["na_k"]), full(lay["na_v"]), cache, cache,
                  pl.BlockSpec((1, 1, tq, tk), lambda bi, hh, t: (hh, kind(t), 0, 0))],
        out_specs=pl.BlockSpec((1, tq, HEAD_DIM), lambda bi, hh, t: (bi, t, hh)),
        out_shape=jax.ShapeDtypeStruct((b, s, h * HEAD_DIM), BF16),
        compiler_params=_params(("parallel", "parallel", "arbitrary"), 40),
    )(qkv, qkv, qkv, ck, cv, bias)


def _rope(x, cos, sin_signed):
    lane = lax.broadcasted_iota(jnp.int32, x.shape, 1)
    partner = jnp.where(lane % 64 < 32, pltpu.roll(x, HEAD_DIM - 32, 1), pltpu.roll(x, 32, 1))
    return x * cos + partner * sin_signed


def _swa_attn_kernel(sink_ref, q_ref, k_ref, v_ref, ck_ref, cv_ref, cos_ref, sin_ref, o_ref, *, seq, rep):
    g, j = pl.program_id(1), pl.program_id(2)
    nwin = 3 * SWA_BLOCK
    start = pl.multiple_of(jnp.clip((j - 1) * SWA_BLOCK, 0, seq - nwin), SWA_BLOCK)
    q0 = pl.multiple_of(j * SWA_BLOCK, SWA_BLOCK)
    cq, sq = cos_ref[pl.ds(q0, SWA_BLOCK), :], sin_ref[pl.ds(q0, SWA_BLOCK), :]
    q = jnp.concatenate(
        [_rope(q_ref[0, :, r * HEAD_DIM:(r + 1) * HEAD_DIM].astype(F32), cq, sq) for r in range(rep)],
        axis=0).astype(BF16)
    kw = _rope(k_ref[0, pl.ds(start, nwin), :].astype(F32),
               cos_ref[pl.ds(start, nwin), :], sin_ref[pl.ds(start, nwin), :]).astype(BF16)
    qpos = q0 + lax.broadcasted_iota(jnp.int32, (rep * SWA_BLOCK, nwin), 0) % SWA_BLOCK
    kpos = start + lax.broadcasted_iota(jnp.int32, (rep * SWA_BLOCK, nwin), 1)
    valid = jnp.abs(qpos - kpos) <= SWA_WINDOW
    s_w = jnp.where(valid, _dot_nt(q, kw) * ATTN_SCALE, NEG_INF)
    s_c = _dot_nt(q, ck_ref[0, 0].astype(BF16)) * ATTN_SCALE
    sink = jnp.concatenate(
        [jnp.full((SWA_BLOCK, 1), sink_ref[g * rep + r], F32) for r in range(rep)], axis=0)
    mx = jnp.maximum(jnp.maximum(jnp.max(s_w, axis=-1, keepdims=True),
                                 jnp.max(s_c, axis=-1, keepdims=True)), sink)
    p_w = jnp.exp(s_w - mx)
    p_c = jnp.exp(s_c - mx)
    den = (jnp.sum(p_w, axis=-1, keepdims=True) + jnp.sum(p_c, axis=-1, keepdims=True)
           + jnp.exp(sink - mx))
    o = (_dot(p_w.astype(BF16), v_ref[0, pl.ds(start, nwin), :])
         + _dot(p_c.astype(BF16), cv_ref[0, 0].astype(BF16))) / den
    for r in range(rep):
        o_ref[0, :, r * HEAD_DIM:(r + 1) * HEAD_DIM] = (
            o[r * SWA_BLOCK:(r + 1) * SWA_BLOCK].astype(o_ref.dtype))


def _rope_tables(seq):
    t = np.arange(seq)
    pos = np.stack([t // GRID_W, t % GRID_W], axis=-1).astype(np.float32)
    nf = HEAD_DIM // 4
    inv_freq = (ROPE_BASE ** (-np.arange(nf, dtype=np.float32) / nf)).astype(np.float32)
    ang = jnp.asarray(pos)[:, :, None] * jnp.asarray(inv_freq)
    cos, sin = jnp.cos(ang), jnp.sin(ang)
    cos_t = jnp.concatenate([cos, cos], axis=-1).reshape(seq, HEAD_DIM)
    sin_t = jnp.concatenate([-sin, sin], axis=-1).reshape(seq, HEAD_DIM)
    return cos_t, sin_t


def _swa_attn(qkv, ck, cv, layer, sink, cos_t, sin_t, lay):
    b, s, _ = qkv.shape
    h, rep = lay["heads"], lay["gqa"]
    kv = h // rep
    p = ck.shape[2]
    assert s % SWA_BLOCK == 0 and s >= 3 * SWA_BLOCK
    full = lambda col: pl.BlockSpec((1, s, HEAD_DIM), lambda bi, g, j: (bi, 0, col + g))
    cache = pl.BlockSpec((1, 1, p, HEAD_DIM), lambda bi, g, j: (bi, layer, 0, g))
    table = pl.BlockSpec((s, HEAD_DIM), lambda bi, g, j: (0, 0))
    qw = rep * HEAD_DIM
    return pl.pallas_call(
        functools.partial(_swa_attn_kernel, seq=s, rep=rep),
        grid=(b, kv, s // SWA_BLOCK),
        in_specs=[pl.BlockSpec(memory_space=pltpu.SMEM),
                  pl.BlockSpec((1, SWA_BLOCK, qw), lambda bi, g, j: (bi, j, lay["sw_q"] // rep + g)),
                  full(lay["sw_k"]), full(lay["sw_v"]), cache, cache, table, table],
        out_specs=pl.BlockSpec((1, SWA_BLOCK, qw), lambda bi, g, j: (bi, j, g)),
        out_shape=jax.ShapeDtypeStruct((b, s, h * HEAD_DIM), BF16),
        compiler_params=_params(("parallel", "parallel", "arbitrary"), 40),
    )(sink, qkv, qkv, qkv, ck, cv, cos_t, sin_t)


def _dft_kernel(c_ref, s_ref, ab_ref, w_ref, o_ref, acc_ref, *, kt, norm):
    k = pl.program_id(2)
    half = ab_ref.shape[2] // 2
    part = _dot(c_ref[...], ab_ref[0, :, :half]) + _dot(s_ref[...], ab_ref[0, :, half:])

    @pl.when(k == 0)
    def _():
        acc_ref[...] = part

    @pl.when(k > 0)
    def _():
        acc_ref[...] += part

    @pl.when(k == kt - 1)
    def _():
        for g in range(w_ref.shape[0]):
            f = (acc_ref[:, g * HEAD_DIM:(g + 1) * HEAD_DIM] * norm).astype(BF16)
            o_ref[0, :, g * HEAD_DIM:(g + 1) * HEAD_DIM] = _dot(f, w_ref[g]).astype(o_ref.dtype)


def _dft_tables(seq):
    jk = (lax.broadcasted_iota(jnp.int32, (seq, seq), 0) * lax.broadcasted_iota(jnp.int32, (seq, seq), 1)) % seq
    ang = jk.astype(F32) * np.float32(2.0 * np.pi / seq)
    return jnp.cos(ang).astype(BF16), (-jnp.sin(ang)).astype(BF16)


def _channel_dft(groups):
    ce = np.outer(np.arange(HEAD_DIM), np.arange(HEAD_DIM)) % HEAD_DIM
    ang = 2.0 * np.pi * ce / HEAD_DIM
    eye = np.eye(groups)
    bd = np.concatenate([np.kron(eye, np.cos(ang)), np.kron(eye, np.sin(ang))], axis=1)
    return jnp.asarray(bd, dtype=BF16)


def _fourier(ab, cos_m, nsin_m, fnet_w):
    b, s, w2 = ab.shape
    w = w2 // 2
    tm, tk = _tile(s, 1024), _tile(s, 1024)
    kt = s // tk
    kern = functools.partial(_dft_kernel, kt=kt, norm=float(1.0 / np.sqrt(s * HEAD_DIM)))
    return pl.pallas_call(
        kern,
        grid=(b, s // tm, kt),
        in_specs=[pl.BlockSpec((tm, tk), lambda bi, i, k: (i, k)),
                  pl.BlockSpec((tm, tk), lambda bi, i, k: (i, k)),
                  pl.BlockSpec((1, tk, w2), lambda bi, i, k: (bi, k, 0)),
                  pl.BlockSpec(fnet_w.shape, lambda bi, i, k: (0, 0, 0))],
        out_specs=pl.BlockSpec((1, tm, w), lambda bi, i, k: (bi, i, 0)),
        out_shape=jax.ShapeDtypeStruct((b, s, w), BF16),
        scratch_shapes=[pltpu.VMEM((tm, w), F32)],
        compiler_params=_params(("parallel", "parallel", "arbitrary"), 48),
    )(cos_m, nsin_m, ab, fnet_w)


def _layout(heads, gqa):
    kv = heads // gqa
    off = np.cumsum([0, heads, heads, heads, heads, heads, kv, kv])
    return dict(heads=heads, gqa=gqa, fu=int(off[0]), na_q=int(off[1]), na_k=int(off[2]), na_v=int(off[3]),
                sw_q=int(off[4]), sw_k=int(off[5]), sw_v=int(off[6]), cols=int(off[7]) * HEAD_DIM)


def kernel(x_prompt, x_sample, c, cache_nat_k, cache_nat_v, cache_swa_k, cache_swa_v, c_ctx, ada_w, ada_b, norm_attn_pre, norm_attn_post, norm_ffn_pre, norm_ffn_post, w_in, na_bias, swa_sink, fnet_w, mix_norm, w_out, ffn_w_up, ffn_conv_w, ffn_conv_b, ffn_w_down):
    batch, seq, d = x_prompt.shape
    dec_batch, dec_seq, _ = x_sample.shape
    depth = ada_w.shape[0]
    heads = na_bias.shape[1]
    kv_heads = cache_swa_k.shape[3]
    past = cache_nat_k.shape[2]
    lay = _layout(heads, heads // kv_heads)
    width = heads * HEAD_DIM
    kvw = kv_heads * HEAD_DIM
    ff = ffn_w_down.shape[1]
    ffp = -(-ff // FF_ALIGN) * FF_ALIGN

    ncond = 1 + dec_batch
    cond = jnp.zeros((-(-ncond // 8) * 8, d), F32).at[0].set(c_ctx).at[1:ncond].set(c)
    mods = _ada(cond, ada_w, ada_b)

    cos_t, sin_t = _rope_tables(dec_seq)
    dft_ctx = _dft_tables(seq)
    dft_lat = _dft_tables(dec_seq)
    chan = _channel_dft(heads)

    src = np.cumsum([0, width, width, width, width, kvw, kvw, width])
    order = np.concatenate([np.arange(src[6], src[7])] + [np.arange(src[n], src[n + 1]) for n in range(6)])

    xp = x_prompt.reshape(batch * seq, d)
    xs = x_sample.reshape(dec_batch * dec_seq, d)
    new_cache = [[], [], [], []]
    for l in range(depth):
        w_in_l = w_in[l][:, order].astype(BF16)
        w_out_l = w_out[l].astype(BF16)
        padc = lambda t: jnp.pad(t, ((0, 0), (0, ffp - ff)))
        w_up_l = jnp.concatenate([padc(ffn_w_up[l][:, :ff]), padc(ffn_w_up[l][:, ff:])], axis=1).astype(BF16)
        conv_w_l = jnp.concatenate([padc(ffn_conv_w[l][:, :ff]), padc(ffn_conv_w[l][:, ff:])], axis=1)
        conv_b_l = jnp.concatenate([padc(ffn_conv_b[l][None, :ff]), padc(ffn_conv_b[l][None, ff:])], axis=1)
        w_down_l = jnp.pad(ffn_w_down[l], ((0, ffp - ff), (0, 0))).astype(BF16)
        fnet_l = fnet_w[l].astype(BF16)
        bias_l = _na_bias_table(na_bias[l], dec_seq // GRID_W)
        sink_ctx = jnp.concatenate([jnp.full((heads,), NEG_INF, F32), swa_sink[l].astype(F32)])

        for is_ctx in (True, False):
            x, s_len, nb = (xp, seq, batch) if is_ctx else (xs, dec_seq, dec_batch)
            rows = slice(0, 1) if is_ctx else slice(1, ncond)
            sh_a, sc_a, g_a, sh_f, sc_f, g_f = [
                mods[l, rows, n * d:(n + 1) * d][:, None, :] for n in range(6)]

            h = _modulate(x, norm_attn_pre[l], sh_a, sc_a, s_len)
            qkv = _mm(h, w_in_l, F32 if is_ctx else BF16)
            qkv3 = qkv.reshape(nb, s_len, lay["cols"])
            if is_ctx:
                o_att = _ctx_attn(qkv3, sink_ctx, lay).reshape(nb * s_len, 2 * width)
                na_part, sw_part = (o_att, 0), (o_att, 1)
                pick = lambda c0, n: qkv3[:, :, c0 * HEAD_DIM:(c0 + n) * HEAD_DIM].reshape(
                    nb, s_len, n, HEAD_DIM)
                for dst, c0, n in zip(new_cache, (lay["na_k"], lay["na_v"], lay["sw_k"], lay["sw_v"]),
                                      (heads, heads, kv_heads, kv_heads)):
                    dst.append(pick(c0, n))
            else:
                ck_na = cache_nat_k.reshape(nb, depth, past, width)
                cv_na = cache_nat_v.reshape(nb, depth, past, width)
                ck_sw = cache_swa_k.reshape(nb, depth, past, kvw)
                cv_sw = cache_swa_v.reshape(nb, depth, past, kvw)
                o_na = _na_attn(qkv3, ck_na, cv_na, l, bias_l, lay).reshape(nb * s_len, width)
                o_sw = _swa_attn(qkv3, ck_sw, cv_sw, l, swa_sink[l].astype(F32), cos_t, sin_t, lay)
                na_part, sw_part = (o_na, 0), (o_sw.reshape(nb * s_len, width), 0)
            ab = _mm(qkv, chan, BF16, tn=2 * width).reshape(nb, s_len, 2 * width)
            cos_m, nsin_m = dft_ctx if is_ctx else dft_lat
            o_fn = _fourier(ab, cos_m, nsin_m, fnet_l).reshape(nb * s_len, width)
            o_cat = _group_norm([na_part, sw_part, (o_fn, 0)], mix_norm[l])
            x = _mm_res(o_cat, w_out_l, x, g_a, norm_attn_post[l], s_len)

            h = _modulate(x, norm_ffn_pre[l], sh_f, sc_f, s_len)
            a = _ffn_up(h, w_up_l, conv_w_l, conv_b_l, s_len)
            x = _mm_res(a, w_down_l, x, g_f, norm_ffn_post[l], s_len)
            if is_ctx:
                xp = x
            else:
                xs = x

    return (xp.reshape(batch, seq, d), xs.reshape(dec_batch, dec_seq, d),
            jnp.stack(new_cache[0], axis=1), jnp.stack(new_cache[1], axis=1),
            jnp.stack(new_cache[2], axis=1), jnp.stack(new_cache[3], axis=1))
```

```python
import functools

import numpy as np
import jax
import jax.numpy as jnp
from jax import lax
from jax.experimental import pallas as pl
from jax.experimental.pallas import tpu as pltpu

GRID_W = 64
HEAD_DIM = 128
NA_WIN_H = 8
NA_WIN_W = 16
NA_TILE_ROWS = 4
NA_KEY_ROWS = NA_TILE_ROWS + NA_WIN_H
SWA_WINDOW = 128
SWA_BLOCK = 128
CONV_W = 3
ROPE_BASE = 10000.0
EPS = 1e-6
NEG_INF = -1e30
ATTN_SCALE = HEAD_DIM ** -0.5
HALO = 16
FF_ALIGN = 512
MXU_WIDTH = 256
MIB = 1024 * 1024
BF16 = jnp.bfloat16
F32 = jnp.float32


def _params(sem, vmem_mib):
    return pltpu.CompilerParams(dimension_semantics=sem, vmem_limit_bytes=vmem_mib * MIB)


def _tile(n, want, align=128):
    if n <= want:
        return n
    t = want - want % align
    while t > align and n % t:
        t -= align
    assert t >= align and n % t == 0, (n, want)
    return t


def _batch_of(i, tm, seq, nb):
    return (i * tm) // seq if nb > 1 else 0


def _dot(a, b):
    return jnp.dot(a, b, preferred_element_type=F32)


def _dot_nt(a, b):
    return lax.dot_general(a, b, (((1,), (1,)), ((), ())), preferred_element_type=F32)


def _ada_kernel(c_ref, w_ref, b_ref, o_ref):
    c = c_ref[...]
    s = (c * (1.0 / (1.0 + jnp.exp(-c)))).astype(BF16)
    o_ref[0] = _dot(s, w_ref[0].astype(BF16)) + b_ref[0]


def _ada(cond, ada_w, ada_b):
    depth, d, n = ada_w.shape
    r = cond.shape[0]
    tn = _tile(n, 512)
    return pl.pallas_call(
        _ada_kernel,
        name="ada",
        grid=(depth, n // tn),
        in_specs=[pl.BlockSpec((r, d), lambda l, j: (0, 0)),
                  pl.BlockSpec((1, d, tn), lambda l, j: (l, 0, j)),
                  pl.BlockSpec((1, 1, tn), lambda l, j: (l, 0, j))],
        out_specs=pl.BlockSpec((1, r, tn), lambda l, j: (l, 0, j)),
        out_shape=jax.ShapeDtypeStruct((depth, r, n), F32),
        compiler_params=_params(("parallel", "parallel"), 40),
    )(cond, ada_w, ada_b.reshape(depth, 1, n))


def _modulate_kernel(x_ref, g_ref, sh_ref, sc_ref, o_ref):
    x = x_ref[...]
    y = x * lax.rsqrt(jnp.mean(x * x, axis=-1, keepdims=True) + EPS) * g_ref[...]
    o_ref[...] = (y * (1.0 + sc_ref[0]) + sh_ref[0]).astype(o_ref.dtype)


def _modulate(x, g, shift, scale, seq):
    m, d = x.shape
    nb = shift.shape[0]
    tm = _tile(m, 256)
    mod_spec = pl.BlockSpec((1, 1, d), lambda i: (_batch_of(i, tm, seq, nb), 0, 0))
    return pl.pallas_call(
        _modulate_kernel,
        name="modulate",
        grid=(m // tm,),
        in_specs=[pl.BlockSpec((tm, d), lambda i: (i, 0)),
                  pl.BlockSpec((1, d), lambda i: (0, 0)), mod_spec, mod_spec],
        out_specs=pl.BlockSpec((tm, d), lambda i: (i, 0)),
        out_shape=jax.ShapeDtypeStruct((m, d), BF16),
        compiler_params=_params(("parallel",), 40),
    )(x, g.reshape(1, d), shift, scale)


def _mm_kernel(a_ref, w_ref, o_ref):
    o_ref[...] = _dot(a_ref[...].astype(BF16), w_ref[...]).astype(o_ref.dtype)


def _mm(a, w, out_dtype, tm=1024, tn=512):
    m = a.shape[0]
    k, n = w.shape
    tm, tn = _tile(m, tm), _tile(n, tn)
    return pl.pallas_call(
        _mm_kernel,
        name="mm",
        grid=(m // tm, n // tn),
        in_specs=[pl.BlockSpec((tm, k), lambda i, j: (i, 0)),
                  pl.BlockSpec((k, tn), lambda i, j: (0, j))],
        out_specs=pl.BlockSpec((tm, tn), lambda i, j: (i, j)),
        out_shape=jax.ShapeDtypeStruct((m, n), out_dtype),
        compiler_params=_params(("parallel", "arbitrary"), 48),
    )(a, w)


def _ffn_up_kernel(prev_ref, a_ref, next_ref, wg_ref, wv_ref, cwg_ref, cwv_ref, cbg_ref, cbv_ref,
                   o_ref, lhs_ref, *, tm, seq, chunk):
    i, j = pl.program_id(0), pl.program_id(1)
    tn = o_ref.shape[1]
    rows = tm + HALO
    edges_in_halo = seq % tm == 0

    @pl.when(j == 0)
    def _():
        lhs_ref[0:tm, :] = a_ref[...]
        is_prev = lax.broadcasted_iota(jnp.int32, (HALO, 1), 0) == HALO - 1
        halo = jnp.where(is_prev, prev_ref[...].astype(F32), next_ref[...].astype(F32))
        if edges_in_halo:
            keep_prev = jnp.where((i * tm) % seq != 0, 1.0, 0.0)
            keep_next = jnp.where(((i + 1) * tm) % seq != 0, 1.0, 0.0)
            halo = halo * jnp.where(is_prev, keep_prev, keep_next)
        lhs_ref[tm:, :] = halo.astype(BF16)

    if not edges_in_halo:
        pos = (i * tm + lax.broadcasted_iota(jnp.int32, (tm, 1), 0)) % seq
        has_prev = pos != 0
        has_next = pos != seq - 1

    def conv(w_ref, cw_ref, cb_ref, cols):
        u = _dot(lhs_ref[...], w_ref[:, cols])
        up = pltpu.roll(u, 1, 0)[0:tm]
        un = pltpu.roll(u, rows - 1, 0)[0:tm]
        if not edges_in_halo:
            up = jnp.where(has_prev, up, 0.0)
            un = jnp.where(has_next, un, 0.0)
        return up * cw_ref[0:1, cols] + u[0:tm] * cw_ref[1:2, cols] + un * cw_ref[2:3, cols] + cb_ref[:, cols]

    chunks = [slice(c0, c0 + chunk) for c0 in range(0, tn, chunk)]
    acts = []
    for cols in chunks:
        gate = conv(wg_ref, cwg_ref, cbg_ref, cols)
        acts.append(gate * (1.0 / (1.0 + jnp.exp(-gate))))
    for cols, act in zip(chunks, acts):
        o_ref[:, cols] = (act * conv(wv_ref, cwv_ref, cbv_ref, cols)).astype(o_ref.dtype)


def _ffn_up(h, w_up, conv_w, conv_b, seq, tm=1024, tn=512):
    m, d = h.shape
    f = w_up.shape[1] // 2
    tm, tn = _tile(m, tm), _tile(f, tn)
    assert seq % tm == 0 or tm % seq == 0
    nj = f // tn
    hb = tm // HALO
    last = m // HALO - 1
    kern = functools.partial(_ffn_up_kernel, tm=tm, seq=seq, chunk=_tile(tn, MXU_WIDTH))
    return pl.pallas_call(
        kern,
        name="ffn_up",
        grid=(m // tm, nj),
        in_specs=[pl.BlockSpec((HALO, d), lambda i, j: (jnp.maximum(i * hb - 1, 0), 0)),
                  pl.BlockSpec((tm, d), lambda i, j: (i, 0), pipeline_mode=pl.Buffered(1)),
                  pl.BlockSpec((HALO, d), lambda i, j: (jnp.minimum((i + 1) * hb, last), 0)),
                  pl.BlockSpec((d, tn), lambda i, j: (0, j)),
                  pl.BlockSpec((d, tn), lambda i, j: (0, j + nj)),
                  pl.BlockSpec((CONV_W, tn), lambda i, j: (0, j)),
                  pl.BlockSpec((CONV_W, tn), lambda i, j: (0, j + nj)),
                  pl.BlockSpec((1, tn), lambda i, j: (0, j)),
                  pl.BlockSpec((1, tn), lambda i, j: (0, j + nj))],
        out_specs=pl.BlockSpec((tm, tn), lambda i, j: (i, j)),
        out_shape=jax.ShapeDtypeStruct((m, f), BF16),
        scratch_shapes=[pltpu.VMEM((tm + HALO, d), BF16)],
        compiler_params=_params(("parallel", "arbitrary"), 56),
    )(h, h, h, w_up, w_up, conv_w, conv_w, conv_b, conv_b)


def _mm_res_kernel(*refs, kt, tc, fuse_next):
    if fuse_next:
        (a_ref, w_ref, x_ref, gate_ref, g_ref, gn_ref, sh_ref, sc_ref,
         o_ref, h_ref, acc_ref, rs_ref, ss_ref) = refs
    else:
        a_ref, w_ref, x_ref, gate_ref, g_ref, o_ref, acc_ref, rs_ref = refs
    k = pl.program_id(1)
    d = acc_ref.shape[1]
    nc = d // tc

    @pl.when(k == 0)
    def _():
        acc_ref[...] = _dot(a_ref[...], w_ref[...])

    @pl.when(jnp.logical_and(k > 0, k < kt))
    def _():
        acc_ref[...] += _dot(a_ref[...], w_ref[...])

    @pl.when(k == kt)
    def _():
        y = acc_ref[...]
        rs_ref[...] = lax.rsqrt(jnp.mean(y * y, axis=-1, keepdims=True) + EPS)

    for c in range(nc):
        cols = slice(c * tc, (c + 1) * tc)

        @pl.when(k == kt + c)
        def _(c=c, cols=cols):
            x_new = x_ref[...] + gate_ref[0] * (acc_ref[:, cols] * rs_ref[...] * g_ref[...])
            o_ref[...] = x_new
            if fuse_next:
                acc_ref[:, cols] = x_new
                part = jnp.sum(x_new * x_new, axis=-1, keepdims=True)
                ss_ref[...] = part if c == 0 else ss_ref[...] + part

        if fuse_next:
            @pl.when(k == kt + nc + c)
            def _(cols=cols):
                r = lax.rsqrt(ss_ref[...] * (1.0 / d) + EPS)
                y = acc_ref[:, cols] * r * gn_ref[...]
                h_ref[...] = (y * (1.0 + sc_ref[0]) + sh_ref[0]).astype(h_ref.dtype)


def _mm_res(a, w, x, gate, g_post, seq, next_mod=None, tm=1024, tk=512, tc=1024):
    m, kdim = a.shape
    d = w.shape[1]
    nb = gate.shape[0]
    tm, tk, tc = _tile(m, tm), _tile(kdim, tk), _tile(d, tc)
    kt, nc = kdim // tk, d // tc
    fuse_next = next_mod is not None
    kk = lambda k: jnp.minimum(k, kt - 1)
    cc = lambda k: jnp.clip(k - kt, 0, nc - 1)
    hc = lambda k: jnp.clip(k - kt - nc, 0, nc - 1)
    bb = lambda i: _batch_of(i, tm, seq, nb)
    in_specs = [pl.BlockSpec((tm, tk), lambda i, k: (i, kk(k))),
                pl.BlockSpec((tk, d), lambda i, k: (kk(k), 0)),
                pl.BlockSpec((tm, tc), lambda i, k: (i, cc(k))),
                pl.BlockSpec((1, 1, tc), lambda i, k: (bb(i), 0, cc(k))),
                pl.BlockSpec((1, tc), lambda i, k: (0, cc(k)))]
    out_specs = [pl.BlockSpec((tm, tc), lambda i, k: (i, cc(k)))]
    out_shape = [jax.ShapeDtypeStruct((m, d), F32)]
    scratch = [pltpu.VMEM((tm, d), F32), pltpu.VMEM((tm, 1), F32)]
    args = [a, w, x, gate, g_post.reshape(1, d)]
    if fuse_next:
        g_next, sh_next, sc_next = next_mod
        mod_spec = pl.BlockSpec((1, 1, tc), lambda i, k: (bb(i), 0, hc(k)))
        in_specs += [pl.BlockSpec((1, tc), lambda i, k: (0, hc(k))), mod_spec, mod_spec]
        out_specs.append(pl.BlockSpec((tm, tc), lambda i, k: (i, hc(k))))
        out_shape.append(jax.ShapeDtypeStruct((m, d), BF16))
        scratch.append(pltpu.VMEM((tm, 1), F32))
        args += [g_next.reshape(1, d), sh_next, sc_next]
    out = pl.pallas_call(
        functools.partial(_mm_res_kernel, kt=kt, tc=tc, fuse_next=fuse_next),
        name="mm_res",
        grid=(m // tm, kt + nc * (2 if fuse_next else 1)),
        in_specs=in_specs,
        out_specs=out_specs,
        out_shape=out_shape,
        scratch_shapes=scratch,
        compiler_params=_params(("parallel", "arbitrary"), 56),
    )(*args)
    return (out[0], out[1]) if fuse_next else (out[0], None)


def _group_norm_kernel(a0_ref, a1_ref, a2_ref, g_ref, o_ref):
    w = a0_ref.shape[1]
    for n, ref in enumerate((a0_ref, a1_ref, a2_ref)):
        a = ref[...].astype(F32)
        y = a * lax.rsqrt(jnp.mean(a * a, axis=-1, keepdims=True) + EPS) * g_ref[:, n * w:(n + 1) * w]
        o_ref[:, n * w:(n + 1) * w] = y.astype(o_ref.dtype)


def _group_norm(parts, gain):
    m = parts[0][0].shape[0]
    w = gain.shape[0] // 3
    tm = _tile(m, 512)
    specs = [pl.BlockSpec((tm, w), lambda i, cb=cb: (i, cb)) for _, cb in parts]
    return pl.pallas_call(
        _group_norm_kernel,
        name="group_norm",
        grid=(m // tm,),
        in_specs=specs + [pl.BlockSpec((1, 3 * w), lambda i: (0, 0))],
        out_specs=pl.BlockSpec((tm, 3 * w), lambda i: (i, 0)),
        out_shape=jax.ShapeDtypeStruct((m, 3 * w), BF16),
        compiler_params=_params(("parallel",), 32),
    )(*[arr for arr, _ in parts], gain.reshape(1, 3 * w))


def _ctx_attn_kernel(sink_ref, q_ref, k_ref, v_ref, o_ref):
    sink = sink_ref[pl.program_id(1)]
    q = q_ref[0].astype(BF16)
    s = _dot_nt(q, k_ref[0].astype(BF16)) * ATTN_SCALE
    mx = jnp.maximum(jnp.max(s, axis=-1, keepdims=True), sink)
    p = jnp.exp(s - mx)
    den = jnp.sum(p, axis=-1, keepdims=True) + jnp.exp(sink - mx)
    o = _dot(p.astype(BF16), v_ref[0].astype(BF16))
    o_ref[0] = (o / den).astype(o_ref.dtype)


def _ctx_attn(qkv, sink, lay):
    b, s, _ = qkv.shape
    h, rep = lay["heads"], lay["gqa"]
    grp = lambda hh: hh >= h
    qcol = lambda hh: jnp.where(grp(hh), lay["sw_q"] + hh - h, lay["na_q"] + hh)
    kcol = lambda hh: jnp.where(grp(hh), lay["sw_k"] + (hh - h) // rep, lay["na_k"] + hh)
    vcol = lambda hh: jnp.where(grp(hh), lay["sw_v"] + (hh - h) // rep, lay["na_v"] + hh)
    blk = lambda col: pl.BlockSpec((1, s, HEAD_DIM), lambda bi, hh: (bi, 0, col(hh)))
    return pl.pallas_call(
        _ctx_attn_kernel,
        name="ctx_attn",
        grid=(b, 2 * h),
        in_specs=[pl.BlockSpec(memory_space=pltpu.SMEM), blk(qcol), blk(kcol), blk(vcol)],
        out_specs=pl.BlockSpec((1, s, HEAD_DIM), lambda bi, hh: (bi, 0, hh)),
        out_shape=jax.ShapeDtypeStruct((b, s, 2 * h * HEAD_DIM), BF16),
        compiler_params=_params(("parallel", "parallel"), 32),
    )(sink, qkv, qkv, qkv)


def _na_attn_kernel(q_ref, k_ref, v_ref, ck_ref, cv_ref, bias_ref, o_ref, *, rows):
    t = pl.program_id(2)
    nkeys = NA_KEY_ROWS * GRID_W
    row0 = jnp.clip(NA_TILE_ROWS * t - NA_WIN_H // 2, 0, rows - NA_KEY_ROWS)
    start = pl.multiple_of(row0 * GRID_W, GRID_W)
    for hh in range(q_ref.shape[2] // HEAD_DIM):
        cols = slice(hh * HEAD_DIM, (hh + 1) * HEAD_DIM)
        q = q_ref[0, :, cols]
        s_w = _dot_nt(q, k_ref[0, pl.ds(start, nkeys), cols]) * ATTN_SCALE + bias_ref[hh, 0]
        s_c = _dot_nt(q, ck_ref[0, 0, :, cols].astype(BF16)) * ATTN_SCALE
        mx = jnp.maximum(jnp.max(s_w, axis=-1, keepdims=True), jnp.max(s_c, axis=-1, keepdims=True))
        p_w = jnp.exp(s_w - mx)
        p_c = jnp.exp(s_c - mx)
        den = jnp.sum(p_w, axis=-1, keepdims=True) + jnp.sum(p_c, axis=-1, keepdims=True)
        o = (_dot(p_w.astype(BF16), v_ref[0, pl.ds(start, nkeys), cols])
             + _dot(p_c.astype(BF16), cv_ref[0, 0, :, cols].astype(BF16)))
        o_ref[0, :, cols] = (o / den).astype(o_ref.dtype)


def _na_bias_table(rel_bias, rows):
    tq = NA_TILE_ROWS
    kinds = ((0, 0), (tq, 0), (rows - tq, rows - NA_KEY_ROWS))
    ql, qc = np.divmod(np.arange(tq * GRID_W), GRID_W)
    kl, kc = np.divmod(np.arange(NA_KEY_ROWS * GRID_W), GRID_W)
    idx, ok = [], []
    for r0, k0 in kinds:
        r = (r0 + ql)[:, None]
        krow = (k0 + kl)[None, :]
        rs = np.clip(r - NA_WIN_H // 2, 0, rows - NA_WIN_H)
        row_ok = (krow >= rs) & (krow < rs + NA_WIN_H)
        cs = np.clip(qc - NA_WIN_W // 2, 0, GRID_W - NA_WIN_W)[:, None]
        col_ok = (kc[None, :] >= cs) & (kc[None, :] < cs + NA_WIN_W)
        drow = np.clip(krow - r + NA_WIN_H - 1, 0, 2 * NA_WIN_H - 2)
        dcol = np.clip(kc[None, :] - qc[:, None] + NA_WIN_W - 1, 0, 2 * NA_WIN_W - 2)
        idx.append(drow * (2 * NA_WIN_W - 1) + dcol)
        ok.append(row_ok & col_ok)
    idx, ok = np.stack(idx), np.stack(ok)
    flat = rel_bias.astype(F32).reshape(rel_bias.shape[0], -1)
    return jnp.where(jnp.asarray(ok)[None], flat[:, idx], NEG_INF)


def _na_attn(qkv, ck, cv, layer, bias, lay):
    b, s, _ = qkv.shape
    h = lay["heads"]
    rows = s // GRID_W
    nt = rows // NA_TILE_ROWS
    assert rows % NA_TILE_ROWS == 0 and rows >= NA_KEY_ROWS and nt >= 2
    tq, tk = NA_TILE_ROWS * GRID_W, NA_KEY_ROWS * GRID_W
    p = ck.shape[2]
    hp = 2 if h % 2 == 0 else 1
    hw = hp * HEAD_DIM
    assert all(lay[n] % hp == 0 for n in ("na_q", "na_k", "na_v"))
    kind = lambda t: jnp.where(t == 0, 0, jnp.where(t == nt - 1, 2, 1))
    full = lambda col: pl.BlockSpec((1, s, hw), lambda bi, hh, t: (bi, 0, col // hp + hh))
    cache = pl.BlockSpec((1, 1, p, hw), lambda bi, hh, t: (bi, layer, 0, hh))
    return pl.pallas_call(
        functools.partial(_na_attn_kernel, rows=rows),
        name="na_attn",
        grid=(b, h // hp, nt),
        in_specs=[pl.BlockSpec((1, tq, hw), lambda bi, hh, t: (bi, t, lay["na_q"] // hp + hh)),
                  full(lay["na_k"]), full(lay["na_v"]), cache, cache,
                  pl.BlockSpec((hp, 1, tq, tk), lambda bi, hh, t: (hh, kind(t), 0, 0))],
        out_specs=pl.BlockSpec((1, tq, hw), lambda bi, hh, t: (bi, t, hh)),
        out_shape=jax.ShapeDtypeStruct((b, s, h * HEAD_DIM), BF16),
        compiler_params=_params(("parallel", "parallel", "arbitrary"), 40),
    )(qkv, qkv, qkv, ck, cv, bias)


def _rope(x, cos, sin_signed):
    lane = lax.broadcasted_iota(jnp.int32, x.shape, 1)
    partner = jnp.where(lane % 64 < 32, pltpu.roll(x, HEAD_DIM - 32, 1), pltpu.roll(x, 32, 1))
    return x * cos + partner * sin_signed


def _swa_attn_kernel(sink_ref, q_ref, k_ref, v_ref, ck_ref, cv_ref, cos_ref, sin_ref, o_ref, *, seq, rep):
    g, j = pl.program_id(1), pl.program_id(2)
    gp = k_ref.shape[2] // HEAD_DIM
    nwin = 3 * SWA_BLOCK
    start = pl.multiple_of(jnp.clip((j - 1) * SWA_BLOCK, 0, seq - nwin), SWA_BLOCK)
    q0 = pl.multiple_of(j * SWA_BLOCK, SWA_BLOCK)
    cq, sq = cos_ref[pl.ds(q0, SWA_BLOCK), :], sin_ref[pl.ds(q0, SWA_BLOCK), :]
    ckw, skw = cos_ref[pl.ds(start, nwin), :], sin_ref[pl.ds(start, nwin), :]
    qpos = q0 + lax.broadcasted_iota(jnp.int32, (SWA_BLOCK, nwin), 0)
    kpos = start + lax.broadcasted_iota(jnp.int32, (SWA_BLOCK, nwin), 1)
    mask = jnp.where(jnp.abs(qpos - kpos) <= SWA_WINDOW, 0.0, NEG_INF)
    head = lambda n: slice(n * HEAD_DIM, (n + 1) * HEAD_DIM)
    for gi in range(gp):
        q = jnp.concatenate(
            [_rope(q_ref[0, :, head(gi * rep + r)].astype(F32), cq, sq) for r in range(rep)],
            axis=0).astype(BF16)
        kw = _rope(k_ref[0, pl.ds(start, nwin), head(gi)].astype(F32), ckw, skw).astype(BF16)
        s_w = ((_dot_nt(q, kw) * ATTN_SCALE).reshape(rep, SWA_BLOCK, nwin) + mask[None]).reshape(
            rep * SWA_BLOCK, nwin)
        s_c = _dot_nt(q, ck_ref[0, 0, :, head(gi)].astype(BF16)) * ATTN_SCALE
        sink = jnp.concatenate(
            [jnp.full((SWA_BLOCK, 1), sink_ref[(g * gp + gi) * rep + r], F32) for r in range(rep)], axis=0)
        mx = jnp.maximum(jnp.maximum(jnp.max(s_w, axis=-1, keepdims=True),
                                     jnp.max(s_c, axis=-1, keepdims=True)), sink)
        p_w = jnp.exp(s_w - mx)
        p_c = jnp.exp(s_c - mx)
        den = (jnp.sum(p_w, axis=-1, keepdims=True) + jnp.sum(p_c, axis=-1, keepdims=True)
               + jnp.exp(sink - mx))
        o = (_dot(p_w.astype(BF16), v_ref[0, pl.ds(start, nwin), head(gi)])
             + _dot(p_c.astype(BF16), cv_ref[0, 0, :, head(gi)].astype(BF16))) / den
        for r in range(rep):
            o_ref[0, :, head(gi * rep + r)] = o[r * SWA_BLOCK:(r + 1) * SWA_BLOCK].astype(o_ref.dtype)


def _rope_tables(seq):
    t = np.arange(seq)
    pos = np.stack([t // GRID_W, t % GRID_W], axis=-1).astype(np.float32)
    nf = HEAD_DIM // 4
    inv_freq = (ROPE_BASE ** (-np.arange(nf, dtype=np.float32) / nf)).astype(np.float32)
    ang = jnp.asarray(pos)[:, :, None] * jnp.asarray(inv_freq)
    cos, sin = jnp.cos(ang), jnp.sin(ang)
    cos_t = jnp.concatenate([cos, cos], axis=-1).reshape(seq, HEAD_DIM)
    sin_t = jnp.concatenate([-sin, sin], axis=-1).reshape(seq, HEAD_DIM)
    return cos_t, sin_t


def _swa_attn(qkv, ck, cv, layer, sink, cos_t, sin_t, lay):
    b, s, _ = qkv.shape
    h, rep = lay["heads"], lay["gqa"]
    kv = h // rep
    p = ck.shape[2]
    assert s % SWA_BLOCK == 0 and s >= 3 * SWA_BLOCK
    gp = 2 if kv % 2 == 0 else 1
    kw = gp * HEAD_DIM
    qw = gp * rep * HEAD_DIM
    assert lay["sw_q"] % (gp * rep) == 0 and lay["sw_k"] % gp == 0 and lay["sw_v"] % gp == 0
    full = lambda col: pl.BlockSpec((1, s, kw), lambda bi, g, j: (bi, 0, col // gp + g))
    cache = pl.BlockSpec((1, 1, p, kw), lambda bi, g, j: (bi, layer, 0, g))
    table = pl.BlockSpec((s, HEAD_DIM), lambda bi, g, j: (0, 0))
    return pl.pallas_call(
        functools.partial(_swa_attn_kernel, seq=s, rep=rep),
        name="swa_attn",
        grid=(b, kv // gp, s // SWA_BLOCK),
        in_specs=[pl.BlockSpec(memory_space=pltpu.SMEM),
                  pl.BlockSpec((1, SWA_BLOCK, qw), lambda bi, g, j: (bi, j, lay["sw_q"] // (gp * rep) + g)),
                  full(lay["sw_k"]), full(lay["sw_v"]), cache, cache, table, table],
        out_specs=pl.BlockSpec((1, SWA_BLOCK, qw), lambda bi, g, j: (bi, j, g)),
        out_shape=jax.ShapeDtypeStruct((b, s, h * HEAD_DIM), BF16),
        compiler_params=_params(("parallel", "parallel", "arbitrary"), 40),
    )(sink, qkv, qkv, qkv, ck, cv, cos_t, sin_t)


def _dft_kernel(c_ref, s_ref, ab_ref, w_ref, o_ref, acc_ref, *, kt, norm):
    k = pl.program_id(2)
    half = ab_ref.shape[2] // 2
    part = _dot(c_ref[...], ab_ref[0, :, :half]) + _dot(s_ref[...], ab_ref[0, :, half:])

    @pl.when(k == 0)
    def _():
        acc_ref[...] = part

    @pl.when(k > 0)
    def _():
        acc_ref[...] += part

    @pl.when(k == kt - 1)
    def _():
        for g in range(w_ref.shape[0]):
            f = (acc_ref[:, g * HEAD_DIM:(g + 1) * HEAD_DIM] * norm).astype(BF16)
            o_ref[0, :, g * HEAD_DIM:(g + 1) * HEAD_DIM] = _dot(f, w_ref[g]).astype(o_ref.dtype)


def _dft_tables(seq):
    r = GRID_W
    assert seq % r == 0
    k = np.arange(seq, dtype=np.int64)
    ang_hi = jnp.asarray(((r * np.arange(seq // r)[:, None] * k) % seq) * (2.0 * np.pi / seq), F32)
    ang_lo = jnp.asarray(((np.arange(r)[:, None] * k) % seq) * (2.0 * np.pi / seq), F32)
    ch, sh = jnp.cos(ang_hi)[:, None, :], jnp.sin(ang_hi)[:, None, :]
    cl, sl = jnp.cos(ang_lo)[None, :, :], jnp.sin(ang_lo)[None, :, :]
    cos_m = (ch * cl - sh * sl).reshape(seq, seq)
    nsin_m = -(sh * cl + ch * sl).reshape(seq, seq)
    return cos_m.astype(BF16), nsin_m.astype(BF16)


def _channel_dft(groups):
    ce = np.outer(np.arange(HEAD_DIM), np.arange(HEAD_DIM)) % HEAD_DIM
    ang = 2.0 * np.pi * ce / HEAD_DIM
    eye = np.eye(groups)
    bd = np.concatenate([np.kron(eye, np.cos(ang)), np.kron(eye, np.sin(ang))], axis=1)
    return jnp.asarray(bd, dtype=BF16)


def _fourier(ab, cos_m, nsin_m, fnet_w):
    b, s, w2 = ab.shape
    w = w2 // 2
    tm, tk = _tile(s, 1024), _tile(s, 1024)
    kt = s // tk
    kern = functools.partial(_dft_kernel, kt=kt, norm=float(1.0 / np.sqrt(s * HEAD_DIM)))
    return pl.pallas_call(
        kern,
        name="fourier",
        grid=(b, s // tm, kt),
        in_specs=[pl.BlockSpec((tm, tk), lambda bi, i, k: (i, k)),
                  pl.BlockSpec((tm, tk), lambda bi, i, k: (i, k)),
                  pl.BlockSpec((1, tk, w2), lambda bi, i, k: (bi, k, 0)),
                  pl.BlockSpec(fnet_w.shape, lambda bi, i, k: (0, 0, 0))],
        out_specs=pl.BlockSpec((1, tm, w), lambda bi, i, k: (bi, i, 0)),
        out_shape=jax.ShapeDtypeStruct((b, s, w), BF16),
        scratch_shapes=[pltpu.VMEM((tm, w), F32)],
        compiler_params=_params(("parallel", "parallel", "arbitrary"), 48),
    )(cos_m, nsin_m, ab, fnet_w)


def _layout(heads, gqa):
    kv = heads // gqa
    off = np.cumsum([0, heads, heads, heads, heads, heads, kv, kv])
    return dict(heads=heads, gqa=gqa, fu=int(off[0]), na_q=int(off[1]), na_k=int(off[2]), na_v=int(off[3]),
                sw_q=int(off[4]), sw_k=int(off[5]), sw_v=int(off[6]), cols=int(off[7]) * HEAD_DIM)


def kernel(x_prompt, x_sample, c, cache_nat_k, cache_nat_v, cache_swa_k, cache_swa_v, c_ctx, ada_w, ada_b, norm_attn_pre, norm_attn_post, norm_ffn_pre, norm_ffn_post, w_in, na_bias, swa_sink, fnet_w, mix_norm, w_out, ffn_w_up, ffn_conv_w, ffn_conv_b, ffn_w_down):
    batch, seq, d = x_prompt.shape
    dec_batch, dec_seq, _ = x_sample.shape
    depth = ada_w.shape[0]
    heads = na_bias.shape[1]
    kv_heads = cache_swa_k.shape[3]
    past = cache_nat_k.shape[2]
    lay = _layout(heads, heads // kv_heads)
    width = heads * HEAD_DIM
    kvw = kv_heads * HEAD_DIM
    ff = ffn_w_down.shape[1]
    ffp = -(-ff // FF_ALIGN) * FF_ALIGN

    ncond = 1 + dec_batch
    cond = jnp.zeros((-(-ncond // 8) * 8, d), F32).at[0].set(c_ctx).at[1:ncond].set(c)
    mods = _ada(cond, ada_w, ada_b)

    cos_t, sin_t = _rope_tables(dec_seq)
    dft_ctx = _dft_tables(seq)
    dft_lat = _dft_tables(dec_seq)
    chan = _channel_dft(heads)

    fu0 = 4 * width + 2 * kvw

    xp = x_prompt.reshape(batch * seq, d)
    xs = x_sample.reshape(dec_batch * dec_seq, d)
    new_cache = [[], [], [], []]
    h_next = {True: None, False: None}
    for l in range(depth):
        w_in_l = jnp.concatenate([w_in[l][:, fu0:], w_in[l][:, :fu0]], axis=1).astype(BF16)
        w_out_l = w_out[l].astype(BF16)
        padc = lambda t: jnp.pad(t, ((0, 0), (0, ffp - ff)))
        w_up_l = jnp.concatenate([padc(ffn_w_up[l][:, :ff]), padc(ffn_w_up[l][:, ff:])], axis=1).astype(BF16)
        conv_w_l = jnp.concatenate([padc(ffn_conv_w[l][:, :ff]), padc(ffn_conv_w[l][:, ff:])], axis=1)
        conv_b_l = jnp.concatenate([padc(ffn_conv_b[l][None, :ff]), padc(ffn_conv_b[l][None, ff:])], axis=1)
        w_down_l = jnp.pad(ffn_w_down[l], ((0, ffp - ff), (0, 0))).astype(BF16)
        fnet_l = fnet_w[l].astype(BF16)
        bias_l = _na_bias_table(na_bias[l], dec_seq // GRID_W)
        sink_ctx = jnp.concatenate([jnp.full((heads,), NEG_INF, F32), swa_sink[l].astype(F32)])

        for is_ctx in (True, False):
            x, s_len, nb = (xp, seq, batch) if is_ctx else (xs, dec_seq, dec_batch)
            rows = slice(0, 1) if is_ctx else slice(1, ncond)
            mod = lambda layer, n: mods[layer, rows, n * d:(n + 1) * d][:, None, :]
            sh_a, sc_a, g_a, sh_f, sc_f, g_f = [mod(l, n) for n in range(6)]

            h = h_next[is_ctx]
            if h is None:
                h = _modulate(x, norm_attn_pre[l], sh_a, sc_a, s_len)
            qkv = _mm(h, w_in_l, F32 if is_ctx else BF16)
            qkv3 = qkv.reshape(nb, s_len, lay["cols"])
            if is_ctx:
                o_att = _ctx_attn(qkv3, sink_ctx, lay).reshape(nb * s_len, 2 * width)
                na_part, sw_part = (o_att, 0), (o_att, 1)
                pick = lambda c0, n: qkv3[:, :, c0 * HEAD_DIM:(c0 + n) * HEAD_DIM].reshape(
                    nb, s_len, n, HEAD_DIM)
                for dst, c0, n in zip(new_cache, (lay["na_k"], lay["na_v"], lay["sw_k"], lay["sw_v"]),
                                      (heads, heads, kv_heads, kv_heads)):
                    dst.append(pick(c0, n))
            else:
                ck_na = cache_nat_k.reshape(nb, depth, past, width)
                cv_na = cache_nat_v.reshape(nb, depth, past, width)
                ck_sw = cache_swa_k.reshape(nb, depth, past, kvw)
                cv_sw = cache_swa_v.reshape(nb, depth, past, kvw)
                o_na = _na_attn(qkv3, ck_na, cv_na, l, bias_l, lay).reshape(nb * s_len, width)
                o_sw = _swa_attn(qkv3, ck_sw, cv_sw, l, swa_sink[l].astype(F32), cos_t, sin_t, lay)
                na_part, sw_part = (o_na, 0), (o_sw.reshape(nb * s_len, width), 0)
            ab = _mm(qkv, chan, BF16, tn=2 * width).reshape(nb, s_len, 2 * width)
            cos_m, nsin_m = dft_ctx if is_ctx else dft_lat
            o_fn = _fourier(ab, cos_m, nsin_m, fnet_l).reshape(nb * s_len, width)
            o_cat = _group_norm([na_part, sw_part, (o_fn, 0)], mix_norm[l])
            x, h = _mm_res(o_cat, w_out_l, x, g_a, norm_attn_post[l], s_len,
                           next_mod=(norm_ffn_pre[l], sh_f, sc_f))
            a = _ffn_up(h, w_up_l, conv_w_l, conv_b_l, s_len)
            nxt = (norm_attn_pre[l + 1], mod(l + 1, 0), mod(l + 1, 1)) if l + 1 < depth else None
            x, h_next[is_ctx] = _mm_res(a, w_down_l, x, g_f, norm_ffn_post[l], s_len, next_mod=nxt)
            if is_ctx:
                xp = x
            else:
                xs = x

    return (xp.reshape(batch, seq, d), xs.reshape(dec_batch, dec_seq, d),
            jnp.stack(new_cache[0], axis=1), jnp.stack(new_cache[1], axis=1),
            jnp.stack(new_cache[2], axis=1), jnp.stack(new_cache[3], axis=1))
```

```python
import functools

import numpy as np
import jax
import jax.numpy as jnp
from jax import lax
from jax.experimental import pallas as pl
from jax.experimental.pallas import tpu as pltpu

GRID_W = 64
HEAD_DIM = 128
NA_WIN_H = 8
NA_WIN_W = 16
NA_TILE_ROWS = 4
NA_KEY_ROWS = NA_TILE_ROWS + NA_WIN_H
SWA_WINDOW = 128
SWA_BLOCK = 128
CONV_W = 3
ROPE_BASE = 10000.0
EPS = 1e-6
NEG_INF = -1e30
ATTN_SCALE = HEAD_DIM ** -0.5
HALO = 16
MXU_WIDTH = 256
MIB = 1024 * 1024
BF16 = jnp.bfloat16
F32 = jnp.float32


def _params(sem, vmem_mib):
    return pltpu.CompilerParams(dimension_semantics=sem, vmem_limit_bytes=vmem_mib * MIB)


def _tile(n, want, align=128):
    if n <= want:
        return n
    t = want - want % align
    while t > align and n % t:
        t -= align
    assert t >= align and n % t == 0, (n, want)
    return t


def _batch_of(i, tm, seq, nb):
    return (i * tm) // seq if nb > 1 else 0


def _dot(a, b):
    return jnp.dot(a, b, preferred_element_type=F32)


def _dot_nt(a, b):
    return lax.dot_general(a, b, (((1,), (1,)), ((), ())), preferred_element_type=F32)


def _ada_kernel(c_ref, w_ref, b_ref, o_ref):
    c = c_ref[...]
    s = (c * (1.0 / (1.0 + jnp.exp(-c)))).astype(BF16)
    o_ref[0] = _dot(s, w_ref[0].astype(BF16)) + b_ref[0]


def _ada(cond, ada_w, ada_b):
    depth, d, n = ada_w.shape
    r = cond.shape[0]
    tn = _tile(n, 512)
    return pl.pallas_call(
        _ada_kernel,
        name="ada",
        grid=(depth, n // tn),
        in_specs=[pl.BlockSpec((r, d), lambda l, j: (0, 0)),
                  pl.BlockSpec((1, d, tn), lambda l, j: (l, 0, j)),
                  pl.BlockSpec((1, 1, tn), lambda l, j: (l, 0, j))],
        out_specs=pl.BlockSpec((1, r, tn), lambda l, j: (l, 0, j)),
        out_shape=jax.ShapeDtypeStruct((depth, r, n), F32),
        compiler_params=_params(("parallel", "parallel"), 40),
    )(cond, ada_w, ada_b.reshape(depth, 1, n))


def _modulate_kernel(x_ref, g_ref, sh_ref, sc_ref, o_ref):
    x = x_ref[...]
    y = x * lax.rsqrt(jnp.mean(x * x, axis=-1, keepdims=True) + EPS) * g_ref[...]
    o_ref[...] = (y * (1.0 + sc_ref[0]) + sh_ref[0]).astype(o_ref.dtype)


def _modulate(x, g, shift, scale, seq):
    m, d = x.shape
    nb = shift.shape[0]
    tm = _tile(m, 256)
    mod_spec = pl.BlockSpec((1, 1, d), lambda i: (_batch_of(i, tm, seq, nb), 0, 0))
    return pl.pallas_call(
        _modulate_kernel,
        name="modulate",
        grid=(m // tm,),
        in_specs=[pl.BlockSpec((tm, d), lambda i: (i, 0)),
                  pl.BlockSpec((1, d), lambda i: (0, 0)), mod_spec, mod_spec],
        out_specs=pl.BlockSpec((tm, d), lambda i: (i, 0)),
        out_shape=jax.ShapeDtypeStruct((m, d), BF16),
        compiler_params=_params(("parallel",), 40),
    )(x, g.reshape(1, d), shift, scale)


def _mm_kernel(a_ref, w_ref, o_ref):
    o_ref[...] = _dot(a_ref[...].astype(BF16), w_ref[...]).astype(o_ref.dtype)


def _mm(a, w, out_dtype, tm=1024, tn=512):
    m = a.shape[0]
    k, n = w.shape
    tm, tn = _tile(m, tm), _tile(n, tn)
    return pl.pallas_call(
        _mm_kernel,
        name="mm",
        grid=(m // tm, n // tn),
        in_specs=[pl.BlockSpec((tm, k), lambda i, j: (i, 0)),
                  pl.BlockSpec((k, tn), lambda i, j: (0, j))],
        out_specs=pl.BlockSpec((tm, tn), lambda i, j: (i, j)),
        out_shape=jax.ShapeDtypeStruct((m, n), out_dtype),
        compiler_params=_params(("parallel", "arbitrary"), 48),
    )(a, w)


def _ffn_up_kernel(prev_ref, a_ref, next_ref, wg_ref, cwg_ref, cbg_ref, *rest, tm, seq, nj, n_last):
    nch = (len(rest) - 2) // 3
    wv_refs, cwv_refs, cbv_refs = rest[:nch], rest[nch:2 * nch], rest[2 * nch:3 * nch]
    o_ref, lhs_ref = rest[3 * nch:]
    i, j = pl.program_id(0), pl.program_id(1)
    chunk = o_ref.shape[1] // nch
    rows = tm + HALO
    edges_in_halo = seq % tm == 0

    @pl.when(j == 0)
    def _():
        lhs_ref[0:tm, :] = a_ref[...]
        is_prev = lax.broadcasted_iota(jnp.int32, (HALO, 1), 0) == HALO - 1
        halo = jnp.where(is_prev, prev_ref[...].astype(F32), next_ref[...].astype(F32))
        if edges_in_halo:
            keep_prev = jnp.where((i * tm) % seq != 0, 1.0, 0.0)
            keep_next = jnp.where(((i + 1) * tm) % seq != 0, 1.0, 0.0)
            halo = halo * jnp.where(is_prev, keep_prev, keep_next)
        lhs_ref[tm:, :] = halo.astype(BF16)

    if not edges_in_halo:
        pos = (i * tm + lax.broadcasted_iota(jnp.int32, (tm, 1), 0)) % seq
        has_prev = pos != 0
        has_next = pos != seq - 1

    def conv(w, cw, cb):
        u = _dot(lhs_ref[...], w)
        up = pltpu.roll(u, 1, 0)[0:tm]
        un = pltpu.roll(u, rows - 1, 0)[0:tm]
        if not edges_in_halo:
            up = jnp.where(has_prev, up, 0.0)
            un = jnp.where(has_next, un, 0.0)
        return up * cw[0:1] + u[0:tm] * cw[1:2] + un * cw[2:3] + cb

    def body(n_valid):
        chunks = [slice(c * chunk, (c + 1) * chunk) for c in range(n_valid)]
        acts = []
        for cols in chunks:
            gate = conv(wg_ref[:, cols], cwg_ref[:, cols], cbg_ref[:, cols])
            acts.append(gate * (1.0 / (1.0 + jnp.exp(-gate))))
        for c, (cols, act) in enumerate(zip(chunks, acts)):
            val = conv(wv_refs[c][...], cwv_refs[c][...], cbv_refs[c][...])
            o_ref[:, cols] = (act * val).astype(o_ref.dtype)

    if n_last == nch:
        body(nch)
    else:
        pl.when(j < nj - 1)(functools.partial(body, nch))
        pl.when(j == nj - 1)(functools.partial(body, n_last))


def _ffn_up(h, w_up, conv_w, conv_b, seq, tm=1024, tn=512):
    m, d = h.shape
    f = w_up.shape[1] // 2
    tm, tn = _tile(m, tm), min(tn, f)
    chunk = min(MXU_WIDTH, tn)
    assert (seq % tm == 0 or tm % seq == 0) and tn % chunk == 0 and f % chunk == 0
    nj, nch = pl.cdiv(f, tn), tn // chunk
    n_last = (f - (nj - 1) * tn) // chunk
    hb = tm // HALO
    last = m // HALO - 1
    top = 2 * f // chunk - 1
    vcol = lambda c: (lambda i, j: (0, jnp.minimum(f // chunk + j * nch + c, top)))
    half = lambda rows_: [pl.BlockSpec((rows_, chunk), vcol(c)) for c in range(nch)]
    kern = functools.partial(_ffn_up_kernel, tm=tm, seq=seq, nj=nj, n_last=n_last)
    return pl.pallas_call(
        kern,
        name="ffn_up",
        grid=(m // tm, nj),
        in_specs=[pl.BlockSpec((HALO, d), lambda i, j: (jnp.maximum(i * hb - 1, 0), 0)),
                  pl.BlockSpec((tm, d), lambda i, j: (i, 0), pipeline_mode=pl.Buffered(1)),
                  pl.BlockSpec((HALO, d), lambda i, j: (jnp.minimum((i + 1) * hb, last), 0)),
                  pl.BlockSpec((d, tn), lambda i, j: (0, j)),
                  pl.BlockSpec((CONV_W, tn), lambda i, j: (0, j)),
                  pl.BlockSpec((1, tn), lambda i, j: (0, j))] + half(d) + half(CONV_W) + half(1),
        out_specs=pl.BlockSpec((tm, tn), lambda i, j: (i, j)),
        out_shape=jax.ShapeDtypeStruct((m, f), BF16),
        scratch_shapes=[pltpu.VMEM((tm + HALO, d), BF16)],
        compiler_params=_params(("parallel", "arbitrary"), 56),
    )(h, h, h, w_up, conv_w, conv_b, *([w_up] * nch), *([conv_w] * nch), *([conv_b] * nch))


def _mm_res_kernel(*refs, kt, k_last, tc, fuse_next):
    if fuse_next:
        (a_ref, w_ref, x_ref, gate_ref, g_ref, gn_ref, sh_ref, sc_ref,
         o_ref, h_ref, acc_ref, rs_ref, ss_ref) = refs
    else:
        a_ref, w_ref, x_ref, gate_ref, g_ref, o_ref, acc_ref, rs_ref = refs
    k = pl.program_id(1)
    d = acc_ref.shape[1]
    nc = d // tc

    def accumulate(first, last):
        kv = k_last if last else a_ref.shape[1]
        y = _dot(a_ref[:, :kv], w_ref[:kv, :])
        if not first:
            y = acc_ref[...] + y
        acc_ref[...] = y
        if last:
            rs_ref[...] = lax.rsqrt(jnp.mean(y * y, axis=-1, keepdims=True) + EPS)

    pl.when(k == 0)(functools.partial(accumulate, True, kt == 1))
    if kt > 2:
        pl.when(jnp.logical_and(k > 0, k < kt - 1))(functools.partial(accumulate, False, False))
    if kt > 1:
        pl.when(k == kt - 1)(functools.partial(accumulate, False, True))

    for c in range(nc):
        cols = slice(c * tc, (c + 1) * tc)

        @pl.when(k == kt + c)
        def _(c=c, cols=cols):
            x_new = x_ref[...] + gate_ref[0] * (acc_ref[:, cols] * rs_ref[...] * g_ref[...])
            o_ref[...] = x_new
            if fuse_next:
                acc_ref[:, cols] = x_new
                part = jnp.sum(x_new * x_new, axis=-1, keepdims=True)
                ss_ref[...] = part if c == 0 else ss_ref[...] + part

        if fuse_next:
            @pl.when(k == kt + nc + c)
            def _(cols=cols):
                r = lax.rsqrt(ss_ref[...] * (1.0 / d) + EPS)
                y = acc_ref[:, cols] * r * gn_ref[...]
                h_ref[...] = (y * (1.0 + sc_ref[0]) + sh_ref[0]).astype(h_ref.dtype)


def _mm_res(a, w, x, gate, g_post, seq, next_mod=None, tm=1024, tk=512, tc=1024):
    m, kdim = a.shape
    d = w.shape[1]
    nb = gate.shape[0]
    tm, tk, tc = _tile(m, tm), min(tk, kdim), _tile(d, tc)
    kt, nc = pl.cdiv(kdim, tk), d // tc
    k_last = kdim - (kt - 1) * tk
    assert k_last % MXU_WIDTH == 0 or kt == 1
    fuse_next = next_mod is not None
    kk = lambda k: jnp.minimum(k, kt - 1)
    cc = lambda k: jnp.clip(k - kt, 0, nc - 1)
    hc = lambda k: jnp.clip(k - kt - nc, 0, nc - 1)
    bb = lambda i: _batch_of(i, tm, seq, nb)
    in_specs = [pl.BlockSpec((tm, tk), lambda i, k: (i, kk(k))),
                pl.BlockSpec((tk, d), lambda i, k: (kk(k), 0)),
                pl.BlockSpec((tm, tc), lambda i, k: (i, cc(k))),
                pl.BlockSpec((1, 1, tc), lambda i, k: (bb(i), 0, cc(k))),
                pl.BlockSpec((1, tc), lambda i, k: (0, cc(k)))]
    out_specs = [pl.BlockSpec((tm, tc), lambda i, k: (i, cc(k)))]
    out_shape = [jax.ShapeDtypeStruct((m, d), F32)]
    scratch = [pltpu.VMEM((tm, d), F32), pltpu.VMEM((tm, 1), F32)]
    args = [a, w, x, gate, g_post.reshape(1, d)]
    if fuse_next:
        g_next, sh_next, sc_next = next_mod
        mod_spec = pl.BlockSpec((1, 1, tc), lambda i, k: (bb(i), 0, hc(k)))
        in_specs += [pl.BlockSpec((1, tc), lambda i, k: (0, hc(k))), mod_spec, mod_spec]
        out_specs.append(pl.BlockSpec((tm, tc), lambda i, k: (i, hc(k))))
        out_shape.append(jax.ShapeDtypeStruct((m, d), BF16))
        scratch.append(pltpu.VMEM((tm, 1), F32))
        args += [g_next.reshape(1, d), sh_next, sc_next]
    out = pl.pallas_call(
        functools.partial(_mm_res_kernel, kt=kt, k_last=k_last, tc=tc, fuse_next=fuse_next),
        name="mm_res",
        grid=(m // tm, kt + nc * (2 if fuse_next else 1)),
        in_specs=in_specs,
        out_specs=out_specs,
        out_shape=out_shape,
        scratch_shapes=scratch,
        compiler_params=_params(("parallel", "arbitrary"), 56),
    )(*args)
    return (out[0], out[1]) if fuse_next else (out[0], None)


def _group_norm_kernel(a0_ref, a1_ref, a2_ref, g_ref, o_ref):
    w = a0_ref.shape[1]
    for n, ref in enumerate((a0_ref, a1_ref, a2_ref)):
        a = ref[...].astype(F32)
        y = a * lax.rsqrt(jnp.mean(a * a, axis=-1, keepdims=True) + EPS) * g_ref[:, n * w:(n + 1) * w]
        o_ref[:, n * w:(n + 1) * w] = y.astype(o_ref.dtype)


def _group_norm(parts, gain):
    m = parts[0][0].shape[0]
    w = gain.shape[0] // 3
    tm = _tile(m, 512)
    specs = [pl.BlockSpec((tm, w), lambda i, cb=cb: (i, cb)) for _, cb in parts]
    return pl.pallas_call(
        _group_norm_kernel,
        name="group_norm",
        grid=(m // tm,),
        in_specs=specs + [pl.BlockSpec((1, 3 * w), lambda i: (0, 0))],
        out_specs=pl.BlockSpec((tm, 3 * w), lambda i: (i, 0)),
        out_shape=jax.ShapeDtypeStruct((m, 3 * w), BF16),
        compiler_params=_params(("parallel",), 32),
    )(*[arr for arr, _ in parts], gain.reshape(1, 3 * w))


def _ctx_attn_kernel(sink_ref, q_ref, k_ref, v_ref, o_ref):
    sink = sink_ref[pl.program_id(1)]
    q = q_ref[0].astype(BF16)
    s = _dot_nt(q, k_ref[0].astype(BF16)) * ATTN_SCALE
    mx = jnp.maximum(jnp.max(s, axis=-1, keepdims=True), sink)
    p = jnp.exp(s - mx)
    den = jnp.sum(p, axis=-1, keepdims=True) + jnp.exp(sink - mx)
    o = _dot(p.astype(BF16), v_ref[0].astype(BF16))
    o_ref[0] = (o / den).astype(o_ref.dtype)


def _ctx_attn(qkv, sink, lay):
    b, s, _ = qkv.shape
    h, rep = lay["heads"], lay["gqa"]
    grp = lambda hh: hh >= h
    qcol = lambda hh: jnp.where(grp(hh), lay["sw_q"] + hh - h, lay["na_q"] + hh)
    kcol = lambda hh: jnp.where(grp(hh), lay["sw_k"] + (hh - h) // rep, lay["na_k"] + hh)
    vcol = lambda hh: jnp.where(grp(hh), lay["sw_v"] + (hh - h) // rep, lay["na_v"] + hh)
    blk = lambda col: pl.BlockSpec((1, s, HEAD_DIM), lambda bi, hh: (bi, 0, col(hh)))
    return pl.pallas_call(
        _ctx_attn_kernel,
        name="ctx_attn",
        grid=(b, 2 * h),
        in_specs=[pl.BlockSpec(memory_space=pltpu.SMEM), blk(qcol), blk(kcol), blk(vcol)],
        out_specs=pl.BlockSpec((1, s, HEAD_DIM), lambda bi, hh: (bi, 0, hh)),
        out_shape=jax.ShapeDtypeStruct((b, s, 2 * h * HEAD_DIM), BF16),
        compiler_params=_params(("parallel", "parallel"), 32),
    )(sink, qkv, qkv, qkv)


def _na_attn_kernel(q_ref, k_ref, v_ref, ck_ref, cv_ref, bias_ref, o_ref, *, rows):
    t = pl.program_id(2)
    nkeys = NA_KEY_ROWS * GRID_W
    row0 = jnp.clip(NA_TILE_ROWS * t - NA_WIN_H // 2, 0, rows - NA_KEY_ROWS)
    start = pl.multiple_of(row0 * GRID_W, GRID_W)
    for hh in range(q_ref.shape[2] // HEAD_DIM):
        cols = slice(hh * HEAD_DIM, (hh + 1) * HEAD_DIM)
        q = q_ref[0, :, cols]
        s_w = _dot_nt(q, k_ref[0, pl.ds(start, nkeys), cols]) * ATTN_SCALE + bias_ref[hh, 0]
        s_c = _dot_nt(q, ck_ref[0, 0, :, cols].astype(BF16)) * ATTN_SCALE
        mx = jnp.maximum(jnp.max(s_w, axis=-1, keepdims=True), jnp.max(s_c, axis=-1, keepdims=True))
        p_w = jnp.exp(s_w - mx)
        p_c = jnp.exp(s_c - mx)
        den = jnp.sum(p_w, axis=-1, keepdims=True) + jnp.sum(p_c, axis=-1, keepdims=True)
        o = (_dot(p_w.astype(BF16), v_ref[0, pl.ds(start, nkeys), cols])
             + _dot(p_c.astype(BF16), cv_ref[0, 0, :, cols].astype(BF16)))
        o_ref[0, :, cols] = (o / den).astype(o_ref.dtype)


def _na_bias_table(rel_bias, rows):
    tq = NA_TILE_ROWS
    nh = rel_bias.shape[0]
    kinds = ((0, 0), (tq, 0), (rows - tq, rows - NA_KEY_ROWS))
    side = GRID_W - NA_WIN_W
    wide = jnp.pad(rel_bias.astype(F32), ((0, 0), (0, 0), (side, side)))
    cols = jnp.stack([wide[:, :, GRID_W - 1 - qc:2 * GRID_W - 1 - qc] for qc in range(GRID_W)], axis=2)
    qc, kc = np.arange(GRID_W)[:, None], np.arange(GRID_W)[None, :]
    cs = np.clip(qc - NA_WIN_W // 2, 0, GRID_W - NA_WIN_W)
    cols = jnp.where(jnp.asarray((kc >= cs) & (kc < cs + NA_WIN_W)), cols, NEG_INF)
    tiles = []
    for r0, k0 in kinds:
        per_row = []
        for ql in range(tq):
            r = r0 + ql
            rs = min(max(r - NA_WIN_H // 2, 0), rows - NA_WIN_H)
            lo, d0 = rs - k0, rs - r + NA_WIN_H - 1
            band = cols[:, d0:d0 + NA_WIN_H]
            per_row.append(jnp.pad(band, ((0, 0), (lo, NA_KEY_ROWS - NA_WIN_H - lo), (0, 0), (0, 0)),
                                   constant_values=NEG_INF))
        tiles.append(jnp.stack(per_row, axis=1))
    table = jnp.stack(tiles, axis=1).transpose(0, 1, 2, 4, 3, 5)
    return table.reshape(nh, len(kinds), tq * GRID_W, NA_KEY_ROWS * GRID_W)


def _na_attn(qkv, ck, cv, layer, bias, lay):
    b, s, _ = qkv.shape
    h = lay["heads"]
    rows = s // GRID_W
    nt = rows // NA_TILE_ROWS
    assert rows % NA_TILE_ROWS == 0 and rows >= NA_KEY_ROWS and nt >= 2
    tq, tk = NA_TILE_ROWS * GRID_W, NA_KEY_ROWS * GRID_W
    p = ck.shape[2]
    hp = 2 if h % 2 == 0 else 1
    hw = hp * HEAD_DIM
    assert all(lay[n] % hp == 0 for n in ("na_q", "na_k", "na_v"))
    kind = lambda t: jnp.where(t == 0, 0, jnp.where(t == nt - 1, 2, 1))
    full = lambda col: pl.BlockSpec((1, s, hw), lambda bi, hh, t: (bi, 0, col // hp + hh))
    cache = pl.BlockSpec((1, 1, p, hw), lambda bi, hh, t: (bi, layer, 0, hh))
    return pl.pallas_call(
        functools.partial(_na_attn_kernel, rows=rows),
        name="na_attn",
        grid=(b, h // hp, nt),
        in_specs=[pl.BlockSpec((1, tq, hw), lambda bi, hh, t: (bi, t, lay["na_q"] // hp + hh)),
                  full(lay["na_k"]), full(lay["na_v"]), cache, cache,
                  pl.BlockSpec((hp, 1, tq, tk), lambda bi, hh, t: (hh, kind(t), 0, 0))],
        out_specs=pl.BlockSpec((1, tq, hw), lambda bi, hh, t: (bi, t, hh)),
        out_shape=jax.ShapeDtypeStruct((b, s, h * HEAD_DIM), BF16),
        compiler_params=_params(("parallel", "parallel", "arbitrary"), 40),
    )(qkv, qkv, qkv, ck, cv, bias)


def _rope(x, cos, sin_signed):
    lane = lax.broadcasted_iota(jnp.int32, x.shape, 1)
    partner = jnp.where(lane % 64 < 32, pltpu.roll(x, HEAD_DIM - 32, 1), pltpu.roll(x, 32, 1))
    return x * cos + partner * sin_signed


def _swa_attn_kernel(sink_ref, q_ref, k_ref, v_ref, ck_ref, cv_ref, cos_ref, sin_ref, o_ref, *, seq, rep):
    g, j = pl.program_id(1), pl.program_id(2)
    gp = k_ref.shape[2] // HEAD_DIM
    nwin = 3 * SWA_BLOCK
    start = pl.multiple_of(jnp.clip((j - 1) * SWA_BLOCK, 0, seq - nwin), SWA_BLOCK)
    q0 = pl.multiple_of(j * SWA_BLOCK, SWA_BLOCK)
    cq, sq = cos_ref[pl.ds(q0, SWA_BLOCK), :], sin_ref[pl.ds(q0, SWA_BLOCK), :]
    ckw, skw = cos_ref[pl.ds(start, nwin), :], sin_ref[pl.ds(start, nwin), :]
    qpos = q0 + lax.broadcasted_iota(jnp.int32, (SWA_BLOCK, nwin), 0)
    kpos = start + lax.broadcasted_iota(jnp.int32, (SWA_BLOCK, nwin), 1)
    mask = jnp.where(jnp.abs(qpos - kpos) <= SWA_WINDOW, 0.0, NEG_INF)
    head = lambda n: slice(n * HEAD_DIM, (n + 1) * HEAD_DIM)
    for gi in range(gp):
        q = jnp.concatenate(
            [_rope(q_ref[0, :, head(gi * rep + r)].astype(F32), cq, sq) for r in range(rep)],
            axis=0).astype(BF16)
        kw = _rope(k_ref[0, pl.ds(start, nwin), head(gi)].astype(F32), ckw, skw).astype(BF16)
        s_w = ((_dot_nt(q, kw) * ATTN_SCALE).reshape(rep, SWA_BLOCK, nwin) + mask[None]).reshape(
            rep * SWA_BLOCK, nwin)
        s_c = _dot_nt(q, ck_ref[0, 0, :, head(gi)].astype(BF16)) * ATTN_SCALE
        sink = jnp.concatenate(
            [jnp.full((SWA_BLOCK, 1), sink_ref[(g * gp + gi) * rep + r], F32) for r in range(rep)], axis=0)
        mx = jnp.maximum(jnp.maximum(jnp.max(s_w, axis=-1, keepdims=True),
                                     jnp.max(s_c, axis=-1, keepdims=True)), sink)
        p_w = jnp.exp(s_w - mx)
        p_c = jnp.exp(s_c - mx)
        den = (jnp.sum(p_w, axis=-1, keepdims=True) + jnp.sum(p_c, axis=-1, keepdims=True)
               + jnp.exp(sink - mx))
        o = (_dot(p_w.astype(BF16), v_ref[0, pl.ds(start, nwin), head(gi)])
             + _dot(p_c.astype(BF16), cv_ref[0, 0, :, head(gi)].astype(BF16))) / den
        for r in range(rep):
            o_ref[0, :, head(gi * rep + r)] = o[r * SWA_BLOCK:(r + 1) * SWA_BLOCK].astype(o_ref.dtype)


def _rope_tables(seq):
    t = np.arange(seq)
    pos = np.stack([t // GRID_W, t % GRID_W], axis=-1).astype(np.float32)
    nf = HEAD_DIM // 4
    inv_freq = (ROPE_BASE ** (-np.arange(nf, dtype=np.float32) / nf)).astype(np.float32)
    ang = jnp.asarray(pos)[:, :, None] * jnp.asarray(inv_freq)
    cos, sin = jnp.cos(ang), jnp.sin(ang)
    cos_t = jnp.concatenate([cos, cos], axis=-1).reshape(seq, HEAD_DIM)
    sin_t = jnp.concatenate([-sin, sin], axis=-1).reshape(seq, HEAD_DIM)
    return cos_t, sin_t


def _swa_attn(qkv, ck, cv, layer, sink, cos_t, sin_t, lay):
    b, s, _ = qkv.shape
    h, rep = lay["heads"], lay["gqa"]
    kv = h // rep
    p = ck.shape[2]
    assert s % SWA_BLOCK == 0 and s >= 3 * SWA_BLOCK
    gp = 2 if kv % 2 == 0 else 1
    kw = gp * HEAD_DIM
    qw = gp * rep * HEAD_DIM
    assert lay["sw_q"] % (gp * rep) == 0 and lay["sw_k"] % gp == 0 and lay["sw_v"] % gp == 0
    full = lambda col: pl.BlockSpec((1, s, kw), lambda bi, g, j: (bi, 0, col // gp + g))
    cache = pl.BlockSpec((1, 1, p, kw), lambda bi, g, j: (bi, layer, 0, g))
    table = pl.BlockSpec((s, HEAD_DIM), lambda bi, g, j: (0, 0))
    return pl.pallas_call(
        functools.partial(_swa_attn_kernel, seq=s, rep=rep),
        name="swa_attn",
        grid=(b, kv // gp, s // SWA_BLOCK),
        in_specs=[pl.BlockSpec(memory_space=pltpu.SMEM),
                  pl.BlockSpec((1, SWA_BLOCK, qw), lambda bi, g, j: (bi, j, lay["sw_q"] // (gp * rep) + g)),
                  full(lay["sw_k"]), full(lay["sw_v"]), cache, cache, table, table],
        out_specs=pl.BlockSpec((1, SWA_BLOCK, qw), lambda bi, g, j: (bi, j, g)),
        out_shape=jax.ShapeDtypeStruct((b, s, h * HEAD_DIM), BF16),
        compiler_params=_params(("parallel", "parallel", "arbitrary"), 40),
    )(sink, qkv, qkv, qkv, ck, cv, cos_t, sin_t)


def _dft_kernel(c_ref, s_ref, ab_ref, w_ref, o_ref, acc_ref, *, kt, norm):
    k = pl.program_id(2)
    half = ab_ref.shape[2] // 2
    part = _dot(c_ref[...], ab_ref[0, :, :half]) + _dot(s_ref[...], ab_ref[0, :, half:])

    @pl.when(k == 0)
    def _():
        acc_ref[...] = part

    @pl.when(k > 0)
    def _():
        acc_ref[...] += part

    @pl.when(k == kt - 1)
    def _():
        for g in range(w_ref.shape[0]):
            f = (acc_ref[:, g * HEAD_DIM:(g + 1) * HEAD_DIM] * norm).astype(BF16)
            o_ref[0, :, g * HEAD_DIM:(g + 1) * HEAD_DIM] = _dot(f, w_ref[g]).astype(o_ref.dtype)


def _dft_tables(seq):
    r = GRID_W
    assert seq % r == 0
    k = np.arange(seq, dtype=np.int64)
    ang_hi = jnp.asarray(((r * np.arange(seq // r)[:, None] * k) % seq) * (2.0 * np.pi / seq), F32)
    ang_lo = jnp.asarray(((np.arange(r)[:, None] * k) % seq) * (2.0 * np.pi / seq), F32)
    ch, sh = jnp.cos(ang_hi)[:, None, :], jnp.sin(ang_hi)[:, None, :]
    cl, sl = jnp.cos(ang_lo)[None, :, :], jnp.sin(ang_lo)[None, :, :]
    cos_m = (ch * cl - sh * sl).reshape(seq, seq)
    nsin_m = -(sh * cl + ch * sl).reshape(seq, seq)
    return cos_m.astype(BF16), nsin_m.astype(BF16)


def _channel_dft(groups):
    ce = np.outer(np.arange(HEAD_DIM), np.arange(HEAD_DIM)) % HEAD_DIM
    ang = 2.0 * np.pi * ce / HEAD_DIM
    eye = np.eye(groups)
    bd = np.concatenate([np.kron(eye, np.cos(ang)), np.kron(eye, np.sin(ang))], axis=1)
    return jnp.asarray(bd, dtype=BF16)


def _fourier(ab, cos_m, nsin_m, fnet_w):
    b, s, w2 = ab.shape
    w = w2 // 2
    tm, tk = _tile(s, 1024), _tile(s, 1024)
    kt = s // tk
    kern = functools.partial(_dft_kernel, kt=kt, norm=float(1.0 / np.sqrt(s * HEAD_DIM)))
    return pl.pallas_call(
        kern,
        name="fourier",
        grid=(b, s // tm, kt),
        in_specs=[pl.BlockSpec((tm, tk), lambda bi, i, k: (i, k)),
                  pl.BlockSpec((tm, tk), lambda bi, i, k: (i, k)),
                  pl.BlockSpec((1, tk, w2), lambda bi, i, k: (bi, k, 0)),
                  pl.BlockSpec(fnet_w.shape, lambda bi, i, k: (0, 0, 0))],
        out_specs=pl.BlockSpec((1, tm, w), lambda bi, i, k: (bi, i, 0)),
        out_shape=jax.ShapeDtypeStruct((b, s, w), BF16),
        scratch_shapes=[pltpu.VMEM((tm, w), F32)],
        compiler_params=_params(("parallel", "parallel", "arbitrary"), 48),
    )(cos_m, nsin_m, ab, fnet_w)


def _layout(heads, gqa):
    kv = heads // gqa
    off = np.cumsum([0, heads, heads, heads, heads, heads, kv, kv])
    return dict(heads=heads, gqa=gqa, fu=int(off[0]), na_q=int(off[1]), na_k=int(off[2]), na_v=int(off[3]),
                sw_q=int(off[4]), sw_k=int(off[5]), sw_v=int(off[6]), cols=int(off[7]) * HEAD_DIM)


def kernel(x_prompt, x_sample, c, cache_nat_k, cache_nat_v, cache_swa_k, cache_swa_v, c_ctx, ada_w, ada_b, norm_attn_pre, norm_attn_post, norm_ffn_pre, norm_ffn_post, w_in, na_bias, swa_sink, fnet_w, mix_norm, w_out, ffn_w_up, ffn_conv_w, ffn_conv_b, ffn_w_down):
    batch, seq, d = x_prompt.shape
    dec_batch, dec_seq, _ = x_sample.shape
    depth = ada_w.shape[0]
    heads = na_bias.shape[1]
    kv_heads = cache_swa_k.shape[3]
    past = cache_nat_k.shape[2]
    lay = _layout(heads, heads // kv_heads)
    width = heads * HEAD_DIM
    kvw = kv_heads * HEAD_DIM

    ncond = 1 + dec_batch
    cond = jnp.zeros((-(-ncond // 8) * 8, d), F32).at[0].set(c_ctx).at[1:ncond].set(c)
    mods = _ada(cond, ada_w, ada_b)

    cos_t, sin_t = _rope_tables(dec_seq)
    dft_ctx = _dft_tables(seq)
    dft_lat = _dft_tables(dec_seq)
    chan = _channel_dft(heads)

    fu0 = 4 * width + 2 * kvw

    xp = x_prompt.reshape(batch * seq, d)
    xs = x_sample.reshape(dec_batch * dec_seq, d)
    new_cache = [[], [], [], []]
    h_next = {True: None, False: None}
    for l in range(depth):
        w_in_l = jnp.concatenate([w_in[l][:, fu0:], w_in[l][:, :fu0]], axis=1).astype(BF16)
        w_out_l = w_out[l].astype(BF16)
        w_up_l = ffn_w_up[l].astype(BF16)
        conv_w_l = ffn_conv_w[l].astype(F32)
        conv_b_l = ffn_conv_b[l].astype(F32)[None]
        w_down_l = ffn_w_down[l].astype(BF16)
        fnet_l = fnet_w[l].astype(BF16)
        bias_l = _na_bias_table(na_bias[l], dec_seq // GRID_W)
        sink_ctx = jnp.concatenate([jnp.full((heads,), NEG_INF, F32), swa_sink[l].astype(F32)])

        for is_ctx in (True, False):
            x, s_len, nb = (xp, seq, batch) if is_ctx else (xs, dec_seq, dec_batch)
            rows = slice(0, 1) if is_ctx else slice(1, ncond)
            mod = lambda layer, n: mods[layer, rows, n * d:(n + 1) * d][:, None, :]
            sh_a, sc_a, g_a, sh_f, sc_f, g_f = [mod(l, n) for n in range(6)]

            h = h_next[is_ctx]
            if h is None:
                h = _modulate(x, norm_attn_pre[l], sh_a, sc_a, s_len)
            qkv = _mm(h, w_in_l, F32 if is_ctx else BF16)
            qkv3 = qkv.reshape(nb, s_len, lay["cols"])
            if is_ctx:
                o_att = _ctx_attn(qkv3, sink_ctx, lay).reshape(nb * s_len, 2 * width)
                na_part, sw_part = (o_att, 0), (o_att, 1)
                pick = lambda c0, n: qkv3[:, :, c0 * HEAD_DIM:(c0 + n) * HEAD_DIM].reshape(
                    nb, s_len, n, HEAD_DIM)
                for dst, c0, n in zip(new_cache, (lay["na_k"], lay["na_v"], lay["sw_k"], lay["sw_v"]),
                                      (heads, heads, kv_heads, kv_heads)):
                    dst.append(pick(c0, n))
            else:
                ck_na = cache_nat_k.reshape(nb, depth, past, width)
                cv_na = cache_nat_v.reshape(nb, depth, past, width)
                ck_sw = cache_swa_k.reshape(nb, depth, past, kvw)
                cv_sw = cache_swa_v.reshape(nb, depth, past, kvw)
                o_na = _na_attn(qkv3, ck_na, cv_na, l, bias_l, lay).reshape(nb * s_len, width)
                o_sw = _swa_attn(qkv3, ck_sw, cv_sw, l, swa_sink[l].astype(F32), cos_t, sin_t, lay)
                na_part, sw_part = (o_na, 0), (o_sw.reshape(nb * s_len, width), 0)
            ab = _mm(qkv, chan, BF16, tn=2 * width).reshape(nb, s_len, 2 * width)
            cos_m, nsin_m = dft_ctx if is_ctx else dft_lat
            o_fn = _fourier(ab, cos_m, nsin_m, fnet_l).reshape(nb * s_len, width)
            o_cat = _group_norm([na_part, sw_part, (o_fn, 0)], mix_norm[l])
            x, h = _mm_res(o_cat, w_out_l, x, g_a, norm_attn_post[l], s_len,
                           next_mod=(norm_ffn_pre[l], sh_f, sc_f))
            a = _ffn_up(h, w_up_l, conv_w_l, conv_b_l, s_len)
            nxt = (norm_attn_pre[l + 1], mod(l + 1, 0), mod(l + 1, 1)) if l + 1 < depth else None
            x, h_next[is_ctx] = _mm_res(a, w_down_l, x, g_f, norm_ffn_post[l], s_len, next_mod=nxt)
            if is_ctx:
                xp = x
            else:
                xs = x

    return (xp.reshape(batch, seq, d), xs.reshape(dec_batch, dec_seq, d),
            jnp.stack(new_cache[0], axis=1), jnp.stack(new_cache[1], axis=1),
            jnp.stack(new_cache[2], axis=1), jnp.stack(new_cache[3], axis=1))
```

```python
import functools

import numpy as np
import jax
import jax.numpy as jnp
from jax import lax
from jax.experimental import pallas as pl
from jax.experimental.pallas import tpu as pltpu

GRID_W = 64
HEAD_DIM = 128
NA_WIN_H = 8
NA_WIN_W = 16
NA_TILE_ROWS = 4
NA_KEY_ROWS = NA_TILE_ROWS + NA_WIN_H
SWA_WINDOW = 128
SWA_BLOCK = 128
CONV_W = 3
ROPE_BASE = 10000.0
EPS = 1e-6
NEG_INF = -1e30
ATTN_SCALE = HEAD_DIM ** -0.5
BF16_ROWS = 16
HALO = BF16_ROWS
MXU_WIDTH = 256
MIB = 1024 * 1024
BF16 = jnp.bfloat16
F32 = jnp.float32


def _params(sem, vmem_mib):
    return pltpu.CompilerParams(dimension_semantics=sem, vmem_limit_bytes=vmem_mib * MIB)


def _tile(n, want, align=128):
    if n <= want:
        return n
    t = want - want % align
    while t > align and n % t:
        t -= align
    assert t >= align and n % t == 0, (n, want)
    return t


def _batch_of(i, tm, seq, nb):
    return (i * tm) // seq if nb > 1 else 0


def _dot(a, b):
    return jnp.dot(a, b, preferred_element_type=F32)


def _dot_nt(a, b):
    return lax.dot_general(a, b, (((1,), (1,)), ((), ())), preferred_element_type=F32)


def _ada_kernel(c_ref, w_ref, b_ref, o_ref):
    c = c_ref[...]
    s = (c * (1.0 / (1.0 + jnp.exp(-c)))).astype(BF16)
    o_ref[0] = _dot(s, w_ref[0].astype(BF16)) + b_ref[0]


def _ada(cond, ada_w, ada_b):
    depth, d, n = ada_w.shape
    r = cond.shape[0]
    tn = _tile(n, 512)
    return pl.pallas_call(
        _ada_kernel,
        name="ada",
        grid=(depth, n // tn),
        in_specs=[pl.BlockSpec((r, d), lambda l, j: (0, 0)),
                  pl.BlockSpec((1, d, tn), lambda l, j: (l, 0, j)),
                  pl.BlockSpec((1, 1, tn), lambda l, j: (l, 0, j))],
        out_specs=pl.BlockSpec((1, r, tn), lambda l, j: (l, 0, j)),
        out_shape=jax.ShapeDtypeStruct((depth, r, n), F32),
        compiler_params=_params(("parallel", "parallel"), 40),
    )(cond, ada_w, ada_b.reshape(depth, 1, n))


def _modulate_kernel(x_ref, g_ref, sh_ref, sc_ref, o_ref):
    x = x_ref[...]
    y = x * lax.rsqrt(jnp.mean(x * x, axis=-1, keepdims=True) + EPS) * g_ref[...]
    o_ref[...] = (y * (1.0 + sc_ref[0]) + sh_ref[0]).astype(o_ref.dtype)


def _modulate(x, g, shift, scale, seq):
    m, d = x.shape
    nb = shift.shape[0]
    tm = _tile(m, 256)
    mod_spec = pl.BlockSpec((1, 1, d), lambda i: (_batch_of(i, tm, seq, nb), 0, 0))
    return pl.pallas_call(
        _modulate_kernel,
        name="modulate",
        grid=(m // tm,),
        in_specs=[pl.BlockSpec((tm, d), lambda i: (i, 0)),
                  pl.BlockSpec((1, d), lambda i: (0, 0)), mod_spec, mod_spec],
        out_specs=pl.BlockSpec((tm, d), lambda i: (i, 0)),
        out_shape=jax.ShapeDtypeStruct((m, d), BF16),
        compiler_params=_params(("parallel",), 40),
    )(x, g.reshape(1, d), shift, scale)


def _mm_kernel(a_ref, w_ref, o_ref):
    o_ref[...] = _dot(a_ref[...].astype(BF16), w_ref[...]).astype(o_ref.dtype)


def _mm(a, w, out_dtype, tm=1024, tn=512, rotate=0):
    m = a.shape[0]
    k, n = w.shape
    tm, tn = _tile(m, tm), _tile(n, tn)
    nj = n // tn
    assert rotate % tn == 0
    return pl.pallas_call(
        _mm_kernel,
        name="mm",
        grid=(m // tm, nj),
        in_specs=[pl.BlockSpec((tm, k), lambda i, j: (i, 0)),
                  pl.BlockSpec((k, tn), lambda i, j: (0, (j + rotate // tn) % nj))],
        out_specs=pl.BlockSpec((tm, tn), lambda i, j: (i, j)),
        out_shape=jax.ShapeDtypeStruct((m, n), out_dtype),
        compiler_params=_params(("parallel", "arbitrary"), 48),
    )(a, w)


def _ffn_up_kernel(prev_ref, a_ref, next_ref, wg_ref, cwg_ref, cbg_ref, *rest, tm, seq, nj, n_last):
    nch = (len(rest) - 2) // 3
    wv_refs, cwv_refs, cbv_refs = rest[:nch], rest[nch:2 * nch], rest[2 * nch:3 * nch]
    o_ref, lhs_ref = rest[3 * nch:]
    i, j = pl.program_id(0), pl.program_id(1)
    chunk = o_ref.shape[1] // nch
    rows = tm + HALO
    edges_in_halo = seq % tm == 0

    @pl.when(j == 0)
    def _():
        lhs_ref[0:tm, :] = a_ref[...]
        is_prev = lax.broadcasted_iota(jnp.int32, (HALO, 1), 0) == HALO - 1
        halo = jnp.where(is_prev, prev_ref[...].astype(F32), next_ref[...].astype(F32))
        if edges_in_halo:
            keep_prev = jnp.where((i * tm) % seq != 0, 1.0, 0.0)
            keep_next = jnp.where(((i + 1) * tm) % seq != 0, 1.0, 0.0)
            halo = halo * jnp.where(is_prev, keep_prev, keep_next)
        lhs_ref[tm:, :] = halo.astype(BF16)

    if not edges_in_halo:
        pos = (i * tm + lax.broadcasted_iota(jnp.int32, (tm, 1), 0)) % seq
        has_prev = pos != 0
        has_next = pos != seq - 1

    def conv(w, cw, cb):
        u = _dot(lhs_ref[...], w)
        up = pltpu.roll(u, 1, 0)[0:tm]
        un = pltpu.roll(u, rows - 1, 0)[0:tm]
        if not edges_in_halo:
            up = jnp.where(has_prev, up, 0.0)
            un = jnp.where(has_next, un, 0.0)
        return up * cw[0:1] + u[0:tm] * cw[1:2] + un * cw[2:3] + cb

    def body(n_valid):
        chunks = [slice(c * chunk, (c + 1) * chunk) for c in range(n_valid)]
        acts = []
        for cols in chunks:
            gate = conv(wg_ref[:, cols], cwg_ref[:, cols], cbg_ref[:, cols])
            acts.append(gate * (1.0 / (1.0 + jnp.exp(-gate))))
        for c, (cols, act) in enumerate(zip(chunks, acts)):
            val = conv(wv_refs[c][...], cwv_refs[c][...], cbv_refs[c][...])
            o_ref[:, cols] = (act * val).astype(o_ref.dtype)

    if n_last == nch:
        body(nch)
    else:
        pl.when(j < nj - 1)(functools.partial(body, nch))
        pl.when(j == nj - 1)(functools.partial(body, n_last))


def _ffn_up(h, w_up, conv_w, conv_b, seq, tm=1024, tn=512):
    m, d = h.shape
    f = w_up.shape[1] // 2
    tm, tn = _tile(m, tm), min(tn, f)
    chunk = min(MXU_WIDTH, tn)
    assert (seq % tm == 0 or tm % seq == 0) and tn % chunk == 0 and f % chunk == 0
    nj, nch = pl.cdiv(f, tn), tn // chunk
    n_last = (f - (nj - 1) * tn) // chunk
    hb = tm // HALO
    last = m // HALO - 1
    top = 2 * f // chunk - 1
    vcol = lambda c: (lambda i, j: (0, jnp.minimum(f // chunk + j * nch + c, top)))
    half = lambda rows_: [pl.BlockSpec((rows_, chunk), vcol(c)) for c in range(nch)]
    kern = functools.partial(_ffn_up_kernel, tm=tm, seq=seq, nj=nj, n_last=n_last)
    return pl.pallas_call(
        kern,
        name="ffn_up",
        grid=(m // tm, nj),
        in_specs=[pl.BlockSpec((HALO, d), lambda i, j: (jnp.maximum(i * hb - 1, 0), 0)),
                  pl.BlockSpec((tm, d), lambda i, j: (i, 0), pipeline_mode=pl.Buffered(1)),
                  pl.BlockSpec((HALO, d), lambda i, j: (jnp.minimum((i + 1) * hb, last), 0)),
                  pl.BlockSpec((d, tn), lambda i, j: (0, j)),
                  pl.BlockSpec((CONV_W, tn), lambda i, j: (0, j)),
                  pl.BlockSpec((1, tn), lambda i, j: (0, j))] + half(d) + half(CONV_W) + half(1),
        out_specs=pl.BlockSpec((tm, tn), lambda i, j: (i, j)),
        out_shape=jax.ShapeDtypeStruct((m, f), BF16),
        scratch_shapes=[pltpu.VMEM((tm + HALO, d), BF16)],
        compiler_params=_params(("parallel", "arbitrary"), 56),
    )(h, h, h, w_up, conv_w, conv_b, *([w_up] * nch), *([conv_w] * nch), *([conv_b] * nch))


def _mm_res_kernel(*refs, kt, k_last, tr, fuse_next):
    if fuse_next:
        a_ref, w_ref, x_ref, gate_ref, g_ref, gn_ref, sh_ref, sc_ref, o_ref, h_ref, acc_ref, rs_ref = refs
    else:
        a_ref, w_ref, x_ref, gate_ref, g_ref, o_ref, acc_ref, rs_ref = refs
    k = pl.program_id(1)

    def accumulate(first, last):
        kv = k_last if last else a_ref.shape[1]
        y = _dot(a_ref[:, :kv], w_ref[:kv, :])
        if not first:
            y = acc_ref[...] + y
        acc_ref[...] = y
        if last:
            rs_ref[...] = lax.rsqrt(jnp.mean(y * y, axis=-1, keepdims=True) + EPS)

    pl.when(k == 0)(functools.partial(accumulate, True, kt == 1))
    if kt > 2:
        pl.when(jnp.logical_and(k > 0, k < kt - 1))(functools.partial(accumulate, False, False))
    if kt > 1:
        pl.when(k == kt - 1)(functools.partial(accumulate, False, True))

    for r in range(acc_ref.shape[0] // tr):
        @pl.when(k == kt + r)
        def _(r=r):
            gain = gate_ref[0] * g_ref[...]
            if fuse_next:
                gain_next, shift_next = gn_ref[...] * (1.0 + sc_ref[0]), sh_ref[0]

            def strip(s, carry):
                lo = pl.multiple_of(s * BF16_ROWS, BF16_ROWS)
                dst, src = pl.ds(lo, BF16_ROWS), pl.ds(r * tr + lo, BF16_ROWS)
                x_new = x_ref[dst, :] + gain * (acc_ref[src, :] * rs_ref[src, :])
                o_ref[dst, :] = x_new
                if fuse_next:
                    inv = lax.rsqrt(jnp.mean(x_new * x_new, axis=-1, keepdims=True) + EPS)
                    h_ref[dst, :] = (x_new * inv * gain_next + shift_next).astype(h_ref.dtype)
                return carry

            lax.fori_loop(0, tr // BF16_ROWS, strip, 0, unroll=4)


def _mm_res(a, w, x, gate, g_post, seq, next_mod=None, tm=1024, tk=512, tr=256):
    m, kdim = a.shape
    d = w.shape[1]
    nb = gate.shape[0]
    tm, tk = _tile(m, tm), min(tk, kdim)
    tr = _tile(tm, tr, align=8)
    kt, nr = pl.cdiv(kdim, tk), tm // tr
    k_last = kdim - (kt - 1) * tk
    assert k_last % MXU_WIDTH == 0 or kt == 1
    fuse_next = next_mod is not None
    kk = lambda k: jnp.minimum(k, kt - 1)
    rr = lambda i, k: i * nr + jnp.clip(k - kt, 0, nr - 1)
    bb = lambda i: _batch_of(i, tm, seq, nb)
    row_spec = pl.BlockSpec((tr, d), lambda i, k: (rr(i, k), 0))
    vec_spec = pl.BlockSpec((1, d), lambda i, k: (0, 0))
    mod_spec = pl.BlockSpec((1, 1, d), lambda i, k: (bb(i), 0, 0))
    in_specs = [pl.BlockSpec((tm, tk), lambda i, k: (i, kk(k))),
                pl.BlockSpec((tk, d), lambda i, k: (kk(k), 0)),
                row_spec, mod_spec, vec_spec]
    out_specs = [row_spec]
    out_shape = [jax.ShapeDtypeStruct((m, d), F32)]
    args = [a, w, x, gate, g_post.reshape(1, d)]
    if fuse_next:
        g_next, sh_next, sc_next = next_mod
        in_specs += [vec_spec, mod_spec, mod_spec]
        out_specs.append(row_spec)
        out_shape.append(jax.ShapeDtypeStruct((m, d), BF16))
        args += [g_next.reshape(1, d), sh_next, sc_next]
    out = pl.pallas_call(
        functools.partial(_mm_res_kernel, kt=kt, k_last=k_last, tr=tr, fuse_next=fuse_next),
        name="mm_res",
        grid=(m // tm, kt + nr),
        in_specs=in_specs,
        out_specs=out_specs,
        out_shape=out_shape,
        scratch_shapes=[pltpu.VMEM((tm, d), F32), pltpu.VMEM((tm, 1), F32)],
        compiler_params=_params(("parallel", "arbitrary"), 56),
    )(*args)
    return (out[0], out[1]) if fuse_next else (out[0], None)


def _group_norm_kernel(a0_ref, a1_ref, a2_ref, g_ref, o_ref):
    w = a0_ref.shape[1]
    for n, ref in enumerate((a0_ref, a1_ref, a2_ref)):
        a = ref[...].astype(F32)
        y = a * lax.rsqrt(jnp.mean(a * a, axis=-1, keepdims=True) + EPS) * g_ref[:, n * w:(n + 1) * w]
        o_ref[:, n * w:(n + 1) * w] = y.astype(o_ref.dtype)


def _group_norm(parts, gain):
    m = parts[0][0].shape[0]
    w = gain.shape[0] // 3
    tm = _tile(m, 512)
    specs = [pl.BlockSpec((tm, w), lambda i, cb=cb: (i, cb)) for _, cb in parts]
    return pl.pallas_call(
        _group_norm_kernel,
        name="group_norm",
        grid=(m // tm,),
        in_specs=specs + [pl.BlockSpec((1, 3 * w), lambda i: (0, 0))],
        out_specs=pl.BlockSpec((tm, 3 * w), lambda i: (i, 0)),
        out_shape=jax.ShapeDtypeStruct((m, 3 * w), BF16),
        compiler_params=_params(("parallel",), 32),
    )(*[arr for arr, _ in parts], gain.reshape(1, 3 * w))


def _ctx_attn_kernel(sink_ref, q_ref, k_ref, v_ref, o_ref):
    sink = sink_ref[pl.program_id(1)]
    q = q_ref[0].astype(BF16)
    s = _dot_nt(q, k_ref[0].astype(BF16)) * ATTN_SCALE
    mx = jnp.maximum(jnp.max(s, axis=-1, keepdims=True), sink)
    p = jnp.exp(s - mx)
    den = jnp.sum(p, axis=-1, keepdims=True) + jnp.exp(sink - mx)
    o = _dot(p.astype(BF16), v_ref[0].astype(BF16))
    o_ref[0] = (o / den).astype(o_ref.dtype)


def _ctx_attn(qkv, sink, lay):
    b, s, _ = qkv.shape
    h, rep = lay["heads"], lay["gqa"]
    grp = lambda hh: hh >= h
    qcol = lambda hh: jnp.where(grp(hh), lay["sw_q"] + hh - h, lay["na_q"] + hh)
    kcol = lambda hh: jnp.where(grp(hh), lay["sw_k"] + (hh - h) // rep, lay["na_k"] + hh)
    vcol = lambda hh: jnp.where(grp(hh), lay["sw_v"] + (hh - h) // rep, lay["na_v"] + hh)
    blk = lambda col: pl.BlockSpec((1, s, HEAD_DIM), lambda bi, hh: (bi, 0, col(hh)))
    return pl.pallas_call(
        _ctx_attn_kernel,
        name="ctx_attn",
        grid=(b, 2 * h),
        in_specs=[pl.BlockSpec(memory_space=pltpu.SMEM), blk(qcol), blk(kcol), blk(vcol)],
        out_specs=pl.BlockSpec((1, s, HEAD_DIM), lambda bi, hh: (bi, 0, hh)),
        out_shape=jax.ShapeDtypeStruct((b, s, 2 * h * HEAD_DIM), BF16),
        compiler_params=_params(("parallel", "parallel"), 32),
    )(sink, qkv, qkv, qkv)


def _na_attn_kernel(q_ref, k_ref, v_ref, ck_ref, cv_ref, bias_ref, o_ref, *, rows):
    t = pl.program_id(2)
    nkeys = NA_KEY_ROWS * GRID_W
    row0 = jnp.clip(NA_TILE_ROWS * t - NA_WIN_H // 2, 0, rows - NA_KEY_ROWS)
    start = pl.multiple_of(row0 * GRID_W, GRID_W)
    for hh in range(q_ref.shape[2] // HEAD_DIM):
        cols = slice(hh * HEAD_DIM, (hh + 1) * HEAD_DIM)
        q = q_ref[0, :, cols]
        s_w = _dot_nt(q, k_ref[0, pl.ds(start, nkeys), cols]) * ATTN_SCALE + bias_ref[hh, 0]
        s_c = _dot_nt(q, ck_ref[0, 0, :, cols].astype(BF16)) * ATTN_SCALE
        mx = jnp.maximum(jnp.max(s_w, axis=-1, keepdims=True), jnp.max(s_c, axis=-1, keepdims=True))
        p_w = jnp.exp(s_w - mx)
        p_c = jnp.exp(s_c - mx)
        den = jnp.sum(p_w, axis=-1, keepdims=True) + jnp.sum(p_c, axis=-1, keepdims=True)
        o = (_dot(p_w.astype(BF16), v_ref[0, pl.ds(start, nkeys), cols])
             + _dot(p_c.astype(BF16), cv_ref[0, 0, :, cols].astype(BF16)))
        o_ref[0, :, cols] = (o / den).astype(o_ref.dtype)


def _na_bias_table(rel_bias, rows):
    tq = NA_TILE_ROWS
    nh = rel_bias.shape[0]
    kinds = ((0, 0), (tq, 0), (rows - tq, rows - NA_KEY_ROWS))
    side = GRID_W - NA_WIN_W
    wide = jnp.pad(rel_bias.astype(F32), ((0, 0), (0, 0), (side, side)))
    cols = jnp.stack([wide[:, :, GRID_W - 1 - qc:2 * GRID_W - 1 - qc] for qc in range(GRID_W)], axis=2)
    qc, kc = np.arange(GRID_W)[:, None], np.arange(GRID_W)[None, :]
    cs = np.clip(qc - NA_WIN_W // 2, 0, GRID_W - NA_WIN_W)
    cols = jnp.where(jnp.asarray((kc >= cs) & (kc < cs + NA_WIN_W)), cols, NEG_INF)
    tiles = []
    for r0, k0 in kinds:
        per_row = []
        for ql in range(tq):
            r = r0 + ql
            rs = min(max(r - NA_WIN_H // 2, 0), rows - NA_WIN_H)
            lo, d0 = rs - k0, rs - r + NA_WIN_H - 1
            band = cols[:, d0:d0 + NA_WIN_H]
            per_row.append(jnp.pad(band, ((0, 0), (lo, NA_KEY_ROWS - NA_WIN_H - lo), (0, 0), (0, 0)),
                                   constant_values=NEG_INF))
        tiles.append(jnp.stack(per_row, axis=1))
    table = jnp.stack(tiles, axis=1).transpose(0, 1, 2, 4, 3, 5)
    return table.reshape(nh, len(kinds), tq * GRID_W, NA_KEY_ROWS * GRID_W)


def _na_attn(qkv, ck, cv, layer, bias, lay):
    b, s, _ = qkv.shape
    h = lay["heads"]
    rows = s // GRID_W
    nt = rows // NA_TILE_ROWS
    assert rows % NA_TILE_ROWS == 0 and rows >= NA_KEY_ROWS and nt >= 2
    tq, tk = NA_TILE_ROWS * GRID_W, NA_KEY_ROWS * GRID_W
    p = ck.shape[2]
    hp = 2 if h % 2 == 0 else 1
    hw = hp * HEAD_DIM
    assert all(lay[n] % hp == 0 for n in ("na_q", "na_k", "na_v"))
    kind = lambda t: jnp.where(t == 0, 0, jnp.where(t == nt - 1, 2, 1))
    full = lambda col: pl.BlockSpec((1, s, hw), lambda bi, hh, t: (bi, 0, col // hp + hh))
    cache = pl.BlockSpec((1, 1, p, hw), lambda bi, hh, t: (bi, layer, 0, hh))
    return pl.pallas_call(
        functools.partial(_na_attn_kernel, rows=rows),
        name="na_attn",
        grid=(b, h // hp, nt),
        in_specs=[pl.BlockSpec((1, tq, hw), lambda bi, hh, t: (bi, t, lay["na_q"] // hp + hh)),
                  full(lay["na_k"]), full(lay["na_v"]), cache, cache,
                  pl.BlockSpec((hp, 1, tq, tk), lambda bi, hh, t: (hh, kind(t), 0, 0))],
        out_specs=pl.BlockSpec((1, tq, hw), lambda bi, hh, t: (bi, t, hh)),
        out_shape=jax.ShapeDtypeStruct((b, s, h * HEAD_DIM), BF16),
        compiler_params=_params(("parallel", "parallel", "arbitrary"), 40),
    )(qkv, qkv, qkv, ck, cv, bias)


def _rope(x, cos, sin_signed):
    lane = lax.broadcasted_iota(jnp.int32, x.shape, 1)
    partner = jnp.where(lane % 64 < 32, pltpu.roll(x, HEAD_DIM - 32, 1), pltpu.roll(x, 32, 1))
    return x * cos + partner * sin_signed


def _swa_attn_kernel(sink_ref, q_ref, k_ref, v_ref, ck_ref, cv_ref, cos_ref, sin_ref, o_ref, *, seq, rep):
    g, j = pl.program_id(1), pl.program_id(2)
    gp = k_ref.shape[2] // HEAD_DIM
    nwin = 3 * SWA_BLOCK
    start = pl.multiple_of(jnp.clip((j - 1) * SWA_BLOCK, 0, seq - nwin), SWA_BLOCK)
    q0 = pl.multiple_of(j * SWA_BLOCK, SWA_BLOCK)
    cq, sq = cos_ref[pl.ds(q0, SWA_BLOCK), :], sin_ref[pl.ds(q0, SWA_BLOCK), :]
    ckw, skw = cos_ref[pl.ds(start, nwin), :], sin_ref[pl.ds(start, nwin), :]
    qpos = q0 + lax.broadcasted_iota(jnp.int32, (SWA_BLOCK, nwin), 0)
    kpos = start + lax.broadcasted_iota(jnp.int32, (SWA_BLOCK, nwin), 1)
    mask = jnp.where(jnp.abs(qpos - kpos) <= SWA_WINDOW, 0.0, NEG_INF)
    head = lambda n: slice(n * HEAD_DIM, (n + 1) * HEAD_DIM)
    for gi in range(gp):
        q = jnp.concatenate(
            [_rope(q_ref[0, :, head(gi * rep + r)].astype(F32), cq, sq) for r in range(rep)],
            axis=0).astype(BF16)
        kw = _rope(k_ref[0, pl.ds(start, nwin), head(gi)].astype(F32), ckw, skw).astype(BF16)
        s_w = ((_dot_nt(q, kw) * ATTN_SCALE).reshape(rep, SWA_BLOCK, nwin) + mask[None]).reshape(
            rep * SWA_BLOCK, nwin)
        s_c = _dot_nt(q, ck_ref[0, 0, :, head(gi)].astype(BF16)) * ATTN_SCALE
        sink = jnp.concatenate(
            [jnp.full((SWA_BLOCK, 1), sink_ref[(g * gp + gi) * rep + r], F32) for r in range(rep)], axis=0)
        mx = jnp.maximum(jnp.maximum(jnp.max(s_w, axis=-1, keepdims=True),
                                     jnp.max(s_c, axis=-1, keepdims=True)), sink)
        p_w = jnp.exp(s_w - mx)
        p_c = jnp.exp(s_c - mx)
        den = (jnp.sum(p_w, axis=-1, keepdims=True) + jnp.sum(p_c, axis=-1, keepdims=True)
               + jnp.exp(sink - mx))
        o = (_dot(p_w.astype(BF16), v_ref[0, pl.ds(start, nwin), head(gi)])
             + _dot(p_c.astype(BF16), cv_ref[0, 0, :, head(gi)].astype(BF16))) / den
        for r in range(rep):
            o_ref[0, :, head(gi * rep + r)] = o[r * SWA_BLOCK:(r + 1) * SWA_BLOCK].astype(o_ref.dtype)


def _rope_tables(seq):
    t = np.arange(seq)
    pos = np.stack([t // GRID_W, t % GRID_W], axis=-1).astype(np.float32)
    nf = HEAD_DIM // 4
    inv_freq = (ROPE_BASE ** (-np.arange(nf, dtype=np.float32) / nf)).astype(np.float32)
    ang = jnp.asarray(pos)[:, :, None] * jnp.asarray(inv_freq)
    cos, sin = jnp.cos(ang), jnp.sin(ang)
    cos_t = jnp.concatenate([cos, cos], axis=-1).reshape(seq, HEAD_DIM)
    sin_t = jnp.concatenate([-sin, sin], axis=-1).reshape(seq, HEAD_DIM)
    return cos_t, sin_t


def _swa_attn(qkv, ck, cv, layer, sink, cos_t, sin_t, lay):
    b, s, _ = qkv.shape
    h, rep = lay["heads"], lay["gqa"]
    kv = h // rep
    p = ck.shape[2]
    assert s % SWA_BLOCK == 0 and s >= 3 * SWA_BLOCK
    gp = 2 if kv % 2 == 0 else 1
    kw = gp * HEAD_DIM
    qw = gp * rep * HEAD_DIM
    assert lay["sw_q"] % (gp * rep) == 0 and lay["sw_k"] % gp == 0 and lay["sw_v"] % gp == 0
    full = lambda col: pl.BlockSpec((1, s, kw), lambda bi, g, j: (bi, 0, col // gp + g))
    cache = pl.BlockSpec((1, 1, p, kw), lambda bi, g, j: (bi, layer, 0, g))
    table = pl.BlockSpec((s, HEAD_DIM), lambda bi, g, j: (0, 0))
    return pl.pallas_call(
        functools.partial(_swa_attn_kernel, seq=s, rep=rep),
        name="swa_attn",
        grid=(b, kv // gp, s // SWA_BLOCK),
        in_specs=[pl.BlockSpec(memory_space=pltpu.SMEM),
                  pl.BlockSpec((1, SWA_BLOCK, qw), lambda bi, g, j: (bi, j, lay["sw_q"] // (gp * rep) + g)),
                  full(lay["sw_k"]), full(lay["sw_v"]), cache, cache, table, table],
        out_specs=pl.BlockSpec((1, SWA_BLOCK, qw), lambda bi, g, j: (bi, j, g)),
        out_shape=jax.ShapeDtypeStruct((b, s, h * HEAD_DIM), BF16),
        compiler_params=_params(("parallel", "parallel", "arbitrary"), 40),
    )(sink, qkv, qkv, qkv, ck, cv, cos_t, sin_t)


def _dft_kernel(c_ref, s_ref, ab_ref, w_ref, o_ref, acc_ref, *, kt, norm):
    k = pl.program_id(2)
    half = ab_ref.shape[2] // 2
    part = _dot(c_ref[...], ab_ref[0, :, :half]) + _dot(s_ref[...], ab_ref[0, :, half:])

    @pl.when(k == 0)
    def _():
        acc_ref[...] = part

    @pl.when(k > 0)
    def _():
        acc_ref[...] += part

    @pl.when(k == kt - 1)
    def _():
        for g in range(w_ref.shape[0]):
            f = (acc_ref[:, g * HEAD_DIM:(g + 1) * HEAD_DIM] * norm).astype(BF16)
            o_ref[0, :, g * HEAD_DIM:(g + 1) * HEAD_DIM] = _dot(f, w_ref[g]).astype(o_ref.dtype)


def _dft_tables(seq):
    r = GRID_W
    assert seq % r == 0
    k = np.arange(seq, dtype=np.int64)
    ang_hi = jnp.asarray(((r * np.arange(seq // r)[:, None] * k) % seq) * (2.0 * np.pi / seq), F32)
    ang_lo = jnp.asarray(((np.arange(r)[:, None] * k) % seq) * (2.0 * np.pi / seq), F32)
    ch, sh = jnp.cos(ang_hi)[:, None, :], jnp.sin(ang_hi)[:, None, :]
    cl, sl = jnp.cos(ang_lo)[None, :, :], jnp.sin(ang_lo)[None, :, :]
    cos_m = (ch * cl - sh * sl).reshape(seq, seq)
    nsin_m = -(sh * cl + ch * sl).reshape(seq, seq)
    return cos_m.astype(BF16), nsin_m.astype(BF16)


def _channel_dft(groups):
    ce = np.outer(np.arange(HEAD_DIM), np.arange(HEAD_DIM)) % HEAD_DIM
    ang = 2.0 * np.pi * ce / HEAD_DIM
    eye = np.eye(groups)
    bd = np.concatenate([np.kron(eye, np.cos(ang)), np.kron(eye, np.sin(ang))], axis=1)
    return jnp.asarray(bd, dtype=BF16)


def _fourier(ab, cos_m, nsin_m, fnet_w):
    b, s, w2 = ab.shape
    w = w2 // 2
    tm, tk = _tile(s, 1024), _tile(s, 1024)
    kt = s // tk
    kern = functools.partial(_dft_kernel, kt=kt, norm=float(1.0 / np.sqrt(s * HEAD_DIM)))
    return pl.pallas_call(
        kern,
        name="fourier",
        grid=(b, s // tm, kt),
        in_specs=[pl.BlockSpec((tm, tk), lambda bi, i, k: (i, k)),
                  pl.BlockSpec((tm, tk), lambda bi, i, k: (i, k)),
                  pl.BlockSpec((1, tk, w2), lambda bi, i, k: (bi, k, 0)),
                  pl.BlockSpec(fnet_w.shape, lambda bi, i, k: (0, 0, 0))],
        out_specs=pl.BlockSpec((1, tm, w), lambda bi, i, k: (bi, i, 0)),
        out_shape=jax.ShapeDtypeStruct((b, s, w), BF16),
        scratch_shapes=[pltpu.VMEM((tm, w), F32)],
        compiler_params=_params(("parallel", "parallel", "arbitrary"), 48),
    )(cos_m, nsin_m, ab, fnet_w)


def _layout(heads, gqa):
    kv = heads // gqa
    off = np.cumsum([0, heads, heads, heads, heads, heads, kv, kv])
    return dict(heads=heads, gqa=gqa, fu=int(off[0]), na_q=int(off[1]), na_k=int(off[2]), na_v=int(off[3]),
                sw_q=int(off[4]), sw_k=int(off[5]), sw_v=int(off[6]), cols=int(off[7]) * HEAD_DIM)


def kernel(x_prompt, x_sample, c, cache_nat_k, cache_nat_v, cache_swa_k, cache_swa_v, c_ctx, ada_w, ada_b, norm_attn_pre, norm_attn_post, norm_ffn_pre, norm_ffn_post, w_in, na_bias, swa_sink, fnet_w, mix_norm, w_out, ffn_w_up, ffn_conv_w, ffn_conv_b, ffn_w_down):
    batch, seq, d = x_prompt.shape
    dec_batch, dec_seq, _ = x_sample.shape
    depth = ada_w.shape[0]
    heads = na_bias.shape[1]
    kv_heads = cache_swa_k.shape[3]
    past = cache_nat_k.shape[2]
    lay = _layout(heads, heads // kv_heads)
    width = heads * HEAD_DIM
    kvw = kv_heads * HEAD_DIM

    ncond = 1 + dec_batch
    cond = jnp.zeros((-(-ncond // 8) * 8, d), F32).at[0].set(c_ctx).at[1:ncond].set(c)
    mods = _ada(cond, ada_w, ada_b)

    cos_t, sin_t = _rope_tables(dec_seq)
    dft_ctx = _dft_tables(seq)
    dft_lat = _dft_tables(dec_seq)
    chan = _channel_dft(heads)

    fu0 = 4 * width + 2 * kvw

    xp = x_prompt.reshape(batch * seq, d)
    xs = x_sample.reshape(dec_batch * dec_seq, d)
    new_cache = [[], [], [], []]
    h_next = {True: None, False: None}
    for l in range(depth):
        w_in_l = w_in[l].astype(BF16)
        w_out_l = w_out[l].astype(BF16)
        w_up_l = ffn_w_up[l].astype(BF16)
        conv_w_l = ffn_conv_w[l].astype(F32)
        conv_b_l = ffn_conv_b[l].astype(F32)[None]
        w_down_l = ffn_w_down[l].astype(BF16)
        fnet_l = fnet_w[l].astype(BF16)
        bias_l = _na_bias_table(na_bias[l], dec_seq // GRID_W)
        sink_ctx = jnp.concatenate([jnp.full((heads,), NEG_INF, F32), swa_sink[l].astype(F32)])

        for is_ctx in (True, False):
            x, s_len, nb = (xp, seq, batch) if is_ctx else (xs, dec_seq, dec_batch)
            rows = slice(0, 1) if is_ctx else slice(1, ncond)
            mod = lambda layer, n: mods[layer, rows, n * d:(n + 1) * d][:, None, :]
            sh_a, sc_a, g_a, sh_f, sc_f, g_f = [mod(l, n) for n in range(6)]

            h = h_next[is_ctx]
            if h is None:
                h = _modulate(x, norm_attn_pre[l], sh_a, sc_a, s_len)
            qkv = _mm(h, w_in_l, F32 if is_ctx else BF16, rotate=fu0)
            qkv3 = qkv.reshape(nb, s_len, lay["cols"])
            if is_ctx:
                o_att = _ctx_attn(qkv3, sink_ctx, lay).reshape(nb * s_len, 2 * width)
                na_part, sw_part = (o_att, 0), (o_att, 1)
                pick = lambda c0, n: qkv3[:, :, c0 * HEAD_DIM:(c0 + n) * HEAD_DIM].reshape(
                    nb, s_len, n, HEAD_DIM)
                for dst, c0, n in zip(new_cache, (lay["na_k"], lay["na_v"], lay["sw_k"], lay["sw_v"]),
                                      (heads, heads, kv_heads, kv_heads)):
                    dst.append(pick(c0, n))
            else:
                ck_na = cache_nat_k.reshape(nb, depth, past, width)
                cv_na = cache_nat_v.reshape(nb, depth, past, width)
                ck_sw = cache_swa_k.reshape(nb, depth, past, kvw)
                cv_sw = cache_swa_v.reshape(nb, depth, past, kvw)
                o_na = _na_attn(qkv3, ck_na, cv_na, l, bias_l, lay).reshape(nb * s_len, width)
                o_sw = _swa_attn(qkv3, ck_sw, cv_sw, l, swa_sink[l].astype(F32), cos_t, sin_t, lay)
                na_part, sw_part = (o_na, 0), (o_sw.reshape(nb * s_len, width), 0)
            ab = _mm(qkv, chan, BF16, tn=2 * width).reshape(nb, s_len, 2 * width)
            cos_m, nsin_m = dft_ctx if is_ctx else dft_lat
            o_fn = _fourier(ab, cos_m, nsin_m, fnet_l).reshape(nb * s_len, width)
            o_cat = _group_norm([na_part, sw_part, (o_fn, 0)], mix_norm[l])
            x, h = _mm_res(o_cat, w_out_l, x, g_a, norm_attn_post[l], s_len,
                           next_mod=(norm_ffn_pre[l], sh_f, sc_f))
            a = _ffn_up(h, w_up_l, conv_w_l, conv_b_l, s_len)
            nxt = (norm_attn_pre[l + 1], mod(l + 1, 0), mod(l + 1, 1)) if l + 1 < depth else None
            x, h_next[is_ctx] = _mm_res(a, w_down_l, x, g_f, norm_ffn_post[l], s_len, next_mod=nxt)
            if is_ctx:
                xp = x
            else:
                xs = x

    return (xp.reshape(batch, seq, d), xs.reshape(dec_batch, dec_seq, d),
            jnp.stack(new_cache[0], axis=1), jnp.stack(new_cache[1], axis=1),
            jnp.stack(new_cache[2], axis=1), jnp.stack(new_cache[3], axis=1))
```

```python
import functools

import numpy as np
import jax
import jax.numpy as jnp
from jax import lax
from jax.experimental import pallas as pl
from jax.experimental.pallas import tpu as pltpu

GRID_W = 64
HEAD_DIM = 128
NA_WIN_H = 8
NA_WIN_W = 16
NA_TILE_ROWS = 4
NA_KEY_ROWS = NA_TILE_ROWS + NA_WIN_H
SWA_WINDOW = 128
SWA_BLOCK = 128
CONV_W = 3
ROPE_BASE = 10000.0
EPS = 1e-6
NEG_INF = -1e30
LOG2E = 1.4426950408889634
ATTN_SCALE = HEAD_DIM ** -0.5 * LOG2E
BF16_ROWS = 16
HALO = BF16_ROWS
MXU_WIDTH = 256
MIB = 1024 * 1024
BF16 = jnp.bfloat16
F32 = jnp.float32


def _params(sem, vmem_mib):
    return pltpu.CompilerParams(dimension_semantics=sem, vmem_limit_bytes=vmem_mib * MIB)


def _tile(n, want, align=128):
    if n <= want:
        return n
    t = want - want % align
    while t > align and n % t:
        t -= align
    assert t >= align and n % t == 0, (n, want)
    return t


def _batch_of(i, tm, seq, nb):
    return (i * tm) // seq if nb > 1 else 0


def _dot(a, b):
    return jnp.dot(a, b, preferred_element_type=F32)


def _dot_nt(a, b):
    return lax.dot_general(a, b, (((1,), (1,)), ((), ())), preferred_element_type=F32)


def _ada_kernel(c_ref, w_ref, b_ref, o_ref):
    c = c_ref[...]
    s = (c * (1.0 / (1.0 + jnp.exp(-c)))).astype(BF16)
    o_ref[0] = _dot(s, w_ref[0].astype(BF16)) + b_ref[0]


def _ada(cond, ada_w, ada_b):
    depth, d, n = ada_w.shape
    r = cond.shape[0]
    tn = _tile(n, 512)
    return pl.pallas_call(
        _ada_kernel,
        name="ada",
        grid=(depth, n // tn),
        in_specs=[pl.BlockSpec((r, d), lambda l, j: (0, 0)),
                  pl.BlockSpec((1, d, tn), lambda l, j: (l, 0, j)),
                  pl.BlockSpec((1, 1, tn), lambda l, j: (l, 0, j))],
        out_specs=pl.BlockSpec((1, r, tn), lambda l, j: (l, 0, j)),
        out_shape=jax.ShapeDtypeStruct((depth, r, n), F32),
        compiler_params=_params(("parallel", "parallel"), 40),
    )(cond, ada_w, ada_b.reshape(depth, 1, n))


def _modulate_kernel(x_ref, g_ref, sh_ref, sc_ref, o_ref):
    x = x_ref[...]
    y = x * lax.rsqrt(jnp.mean(x * x, axis=-1, keepdims=True) + EPS) * g_ref[...]
    o_ref[...] = (y * (1.0 + sc_ref[0]) + sh_ref[0]).astype(o_ref.dtype)


def _modulate(x, g, shift, scale, seq):
    m, d = x.shape
    nb = shift.shape[0]
    tm = _tile(m, 256)
    mod_spec = pl.BlockSpec((1, 1, d), lambda i: (_batch_of(i, tm, seq, nb), 0, 0))
    return pl.pallas_call(
        _modulate_kernel,
        name="modulate",
        grid=(m // tm,),
        in_specs=[pl.BlockSpec((tm, d), lambda i: (i, 0)),
                  pl.BlockSpec((1, d), lambda i: (0, 0)), mod_spec, mod_spec],
        out_specs=pl.BlockSpec((tm, d), lambda i: (i, 0)),
        out_shape=jax.ShapeDtypeStruct((m, d), BF16),
        compiler_params=_params(("parallel",), 40),
    )(x, g.reshape(1, d), shift, scale)


def _mm_kernel(a_ref, w_ref, o_ref):
    o_ref[...] = _dot(a_ref[...].astype(BF16), w_ref[...]).astype(o_ref.dtype)


def _mm(a, w, out_dtype, tm=1024, tn=512, rotate=0):
    m = a.shape[0]
    k, n = w.shape
    tm, tn = _tile(m, tm), _tile(n, tn)
    nj = n // tn
    assert rotate % tn == 0
    return pl.pallas_call(
        _mm_kernel,
        name="mm",
        grid=(m // tm, nj),
        in_specs=[pl.BlockSpec((tm, k), lambda i, j: (i, 0)),
                  pl.BlockSpec((k, tn), lambda i, j: (0, (j + rotate // tn) % nj))],
        out_specs=pl.BlockSpec((tm, tn), lambda i, j: (i, j)),
        out_shape=jax.ShapeDtypeStruct((m, n), out_dtype),
        compiler_params=_params(("parallel", "arbitrary"), 48),
    )(a, w)


def _ffn_up_kernel(prev_ref, a_ref, next_ref, wg_ref, cwg_ref, cbg_ref, *rest, tm, seq, nj, n_last):
    nch = (len(rest) - 2) // 3
    wv_refs, cwv_refs, cbv_refs = rest[:nch], rest[nch:2 * nch], rest[2 * nch:3 * nch]
    o_ref, lhs_ref = rest[3 * nch:]
    i, j = pl.program_id(0), pl.program_id(1)
    chunk = o_ref.shape[1] // nch
    rows = tm + HALO
    edges_in_halo = seq % tm == 0

    @pl.when(j == 0)
    def _():
        lhs_ref[0:tm, :] = a_ref[...]
        is_prev = lax.broadcasted_iota(jnp.int32, (HALO, 1), 0) == HALO - 1
        halo = jnp.where(is_prev, prev_ref[...].astype(F32), next_ref[...].astype(F32))
        if edges_in_halo:
            keep_prev = jnp.where((i * tm) % seq != 0, 1.0, 0.0)
            keep_next = jnp.where(((i + 1) * tm) % seq != 0, 1.0, 0.0)
            halo = halo * jnp.where(is_prev, keep_prev, keep_next)
        lhs_ref[tm:, :] = halo.astype(BF16)

    if not edges_in_halo:
        pos = (i * tm + lax.broadcasted_iota(jnp.int32, (tm, 1), 0)) % seq
        has_prev = pos != 0
        has_next = pos != seq - 1

    def conv(w, cw, cb):
        u = _dot(lhs_ref[...], w)
        up = pltpu.roll(u, 1, 0)[0:tm]
        un = pltpu.roll(u, rows - 1, 0)[0:tm]
        if not edges_in_halo:
            up = jnp.where(has_prev, up, 0.0)
            un = jnp.where(has_next, un, 0.0)
        return up * cw[0:1] + u[0:tm] * cw[1:2] + un * cw[2:3] + cb

    def body(n_valid):
        chunks = [slice(c * chunk, (c + 1) * chunk) for c in range(n_valid)]
        acts = []
        for cols in chunks:
            gate = conv(wg_ref[:, cols], cwg_ref[:, cols], cbg_ref[:, cols])
            acts.append(gate * (1.0 / (1.0 + jnp.exp(-gate))))
        for c, (cols, act) in enumerate(zip(chunks, acts)):
            val = conv(wv_refs[c][...], cwv_refs[c][...], cbv_refs[c][...])
            o_ref[:, cols] = (act * val).astype(o_ref.dtype)

    if n_last == nch:
        body(nch)
    else:
        pl.when(j < nj - 1)(functools.partial(body, nch))
        pl.when(j == nj - 1)(functools.partial(body, n_last))


def _ffn_up(h, w_up, conv_w, conv_b, seq, tm=1024, tn=512):
    m, d = h.shape
    f = w_up.shape[1] // 2
    tm, tn = _tile(m, tm), min(tn, f)
    chunk = min(MXU_WIDTH, tn)
    assert (seq % tm == 0 or tm % seq == 0) and tn % chunk == 0 and f % chunk == 0
    nj, nch = pl.cdiv(f, tn), tn // chunk
    n_last = (f - (nj - 1) * tn) // chunk
    hb = tm // HALO
    last = m // HALO - 1
    top = 2 * f // chunk - 1
    vcol = lambda c: (lambda i, j: (0, jnp.minimum(f // chunk + j * nch + c, top)))
    half = lambda rows_: [pl.BlockSpec((rows_, chunk), vcol(c)) for c in range(nch)]
    kern = functools.partial(_ffn_up_kernel, tm=tm, seq=seq, nj=nj, n_last=n_last)
    return pl.pallas_call(
        kern,
        name="ffn_up",
        grid=(m // tm, nj),
        in_specs=[pl.BlockSpec((HALO, d), lambda i, j: (jnp.maximum(i * hb - 1, 0), 0)),
                  pl.BlockSpec((tm, d), lambda i, j: (i, 0), pipeline_mode=pl.Buffered(1)),
                  pl.BlockSpec((HALO, d), lambda i, j: (jnp.minimum((i + 1) * hb, last), 0)),
                  pl.BlockSpec((d, tn), lambda i, j: (0, j)),
                  pl.BlockSpec((CONV_W, tn), lambda i, j: (0, j)),
                  pl.BlockSpec((1, tn), lambda i, j: (0, j))] + half(d) + half(CONV_W) + half(1),
        out_specs=pl.BlockSpec((tm, tn), lambda i, j: (i, j)),
        out_shape=jax.ShapeDtypeStruct((m, f), BF16),
        scratch_shapes=[pltpu.VMEM((tm + HALO, d), BF16)],
        compiler_params=_params(("parallel", "arbitrary"), 56),
    )(h, h, h, w_up, conv_w, conv_b, *([w_up] * nch), *([conv_w] * nch), *([conv_b] * nch))


def _mm_res_kernel(*refs, kt, k_last, tr, fuse_next):
    if fuse_next:
        a_ref, w_ref, x_ref, gate_ref, g_ref, gn_ref, sh_ref, sc_ref, o_ref, h_ref, acc_ref, rs_ref = refs
    else:
        a_ref, w_ref, x_ref, gate_ref, g_ref, o_ref, acc_ref, rs_ref = refs
    k = pl.program_id(1)

    def accumulate(first, last):
        kv = k_last if last else a_ref.shape[1]
        y = _dot(a_ref[:, :kv], w_ref[:kv, :])
        if not first:
            y = acc_ref[...] + y
        acc_ref[...] = y
        if last:
            rs_ref[...] = lax.rsqrt(jnp.mean(y * y, axis=-1, keepdims=True) + EPS)

    pl.when(k == 0)(functools.partial(accumulate, True, kt == 1))
    if kt > 2:
        pl.when(jnp.logical_and(k > 0, k < kt - 1))(functools.partial(accumulate, False, False))
    if kt > 1:
        pl.when(k == kt - 1)(functools.partial(accumulate, False, True))

    for r in range(acc_ref.shape[0] // tr):
        @pl.when(k == kt + r)
        def _(r=r):
            gain = gate_ref[0] * g_ref[...]
            if fuse_next:
                gain_next, shift_next = gn_ref[...] * (1.0 + sc_ref[0]), sh_ref[0]

            def strip(s, carry):
                lo = pl.multiple_of(s * BF16_ROWS, BF16_ROWS)
                dst, src = pl.ds(lo, BF16_ROWS), pl.ds(r * tr + lo, BF16_ROWS)
                x_new = x_ref[dst, :] + gain * (acc_ref[src, :] * rs_ref[src, :])
                o_ref[dst, :] = x_new
                if fuse_next:
                    inv = lax.rsqrt(jnp.mean(x_new * x_new, axis=-1, keepdims=True) + EPS)
                    h_ref[dst, :] = (x_new * inv * gain_next + shift_next).astype(h_ref.dtype)
                return carry

            lax.fori_loop(0, tr // BF16_ROWS, strip, 0, unroll=4)


def _mm_res(a, w, x, gate, g_post, seq, next_mod=None, tm=1024, tk=512, tr=256):
    m, kdim = a.shape
    d = w.shape[1]
    nb = gate.shape[0]
    tm, tk = _tile(m, tm), min(tk, kdim)
    tr = _tile(tm, tr, align=8)
    kt, nr = pl.cdiv(kdim, tk), tm // tr
    k_last = kdim - (kt - 1) * tk
    assert k_last % MXU_WIDTH == 0 or kt == 1
    fuse_next = next_mod is not None
    kk = lambda k: jnp.minimum(k, kt - 1)
    rr = lambda i, k: i * nr + jnp.clip(k - kt, 0, nr - 1)
    bb = lambda i: _batch_of(i, tm, seq, nb)
    row_spec = pl.BlockSpec((tr, d), lambda i, k: (rr(i, k), 0))
    vec_spec = pl.BlockSpec((1, d), lambda i, k: (0, 0))
    mod_spec = pl.BlockSpec((1, 1, d), lambda i, k: (bb(i), 0, 0))
    in_specs = [pl.BlockSpec((tm, tk), lambda i, k: (i, kk(k))),
                pl.BlockSpec((tk, d), lambda i, k: (kk(k), 0)),
                row_spec, mod_spec, vec_spec]
    out_specs = [row_spec]
    out_shape = [jax.ShapeDtypeStruct((m, d), F32)]
    args = [a, w, x, gate, g_post.reshape(1, d)]
    if fuse_next:
        g_next, sh_next, sc_next = next_mod
        in_specs += [vec_spec, mod_spec, mod_spec]
        out_specs.append(row_spec)
        out_shape.append(jax.ShapeDtypeStruct((m, d), BF16))
        args += [g_next.reshape(1, d), sh_next, sc_next]
    out = pl.pallas_call(
        functools.partial(_mm_res_kernel, kt=kt, k_last=k_last, tr=tr, fuse_next=fuse_next),
        name="mm_res",
        grid=(m // tm, kt + nr),
        in_specs=in_specs,
        out_specs=out_specs,
        out_shape=out_shape,
        scratch_shapes=[pltpu.VMEM((tm, d), F32), pltpu.VMEM((tm, 1), F32)],
        compiler_params=_params(("parallel", "arbitrary"), 56),
    )(*args)
    return (out[0], out[1]) if fuse_next else (out[0], None)


def _group_norm_kernel(a0_ref, a1_ref, a2_ref, g_ref, o_ref):
    w = a0_ref.shape[1]
    for n, ref in enumerate((a0_ref, a1_ref, a2_ref)):
        a = ref[...].astype(F32)
        y = a * lax.rsqrt(jnp.mean(a * a, axis=-1, keepdims=True) + EPS) * g_ref[:, n * w:(n + 1) * w]
        o_ref[:, n * w:(n + 1) * w] = y.astype(o_ref.dtype)


def _group_norm(parts, gain):
    m = parts[0][0].shape[0]
    w = gain.shape[0] // 3
    tm = _tile(m, 512)
    specs = [pl.BlockSpec((tm, w), lambda i, cb=cb: (i, cb)) for _, cb in parts]
    return pl.pallas_call(
        _group_norm_kernel,
        name="group_norm",
        grid=(m // tm,),
        in_specs=specs + [pl.BlockSpec((1, 3 * w), lambda i: (0, 0))],
        out_specs=pl.BlockSpec((tm, 3 * w), lambda i: (i, 0)),
        out_shape=jax.ShapeDtypeStruct((m, 3 * w), BF16),
        compiler_params=_params(("parallel",), 32),
    )(*[arr for arr, _ in parts], gain.reshape(1, 3 * w))


def _ctx_attn_kernel(sink_ref, q_ref, k_ref, v_ref, o_ref):
    sink = sink_ref[pl.program_id(1)] * LOG2E
    q = q_ref[0].astype(BF16)
    s = _dot_nt(q, k_ref[0].astype(BF16)) * ATTN_SCALE
    mx = jnp.maximum(jnp.max(s, axis=-1, keepdims=True), sink)
    p = jnp.exp2(s - mx)
    den = jnp.sum(p, axis=-1, keepdims=True) + jnp.exp2(sink - mx)
    o = _dot(p.astype(BF16), v_ref[0].astype(BF16))
    o_ref[0] = (o / den).astype(o_ref.dtype)


def _ctx_attn(qkv, sink, lay):
    b, s, _ = qkv.shape
    h, rep = lay["heads"], lay["gqa"]
    grp = lambda hh: hh >= h
    qcol = lambda hh: jnp.where(grp(hh), lay["sw_q"] + hh - h, lay["na_q"] + hh)
    kcol = lambda hh: jnp.where(grp(hh), lay["sw_k"] + (hh - h) // rep, lay["na_k"] + hh)
    vcol = lambda hh: jnp.where(grp(hh), lay["sw_v"] + (hh - h) // rep, lay["na_v"] + hh)
    blk = lambda col: pl.BlockSpec((1, s, HEAD_DIM), lambda bi, hh: (bi, 0, col(hh)))
    return pl.pallas_call(
        _ctx_attn_kernel,
        name="ctx_attn",
        grid=(b, 2 * h),
        in_specs=[pl.BlockSpec(memory_space=pltpu.SMEM), blk(qcol), blk(kcol), blk(vcol)],
        out_specs=pl.BlockSpec((1, s, HEAD_DIM), lambda bi, hh: (bi, 0, hh)),
        out_shape=jax.ShapeDtypeStruct((b, s, 2 * h * HEAD_DIM), BF16),
        compiler_params=_params(("parallel", "parallel"), 32),
    )(sink, qkv, qkv, qkv)


def _na_attn_kernel(q_ref, k_ref, v_ref, ck_ref, cv_ref, bias_ref, o_ref, *, rows):
    t = pl.program_id(2)
    nkeys = NA_KEY_ROWS * GRID_W
    row0 = jnp.clip(NA_TILE_ROWS * t - NA_WIN_H // 2, 0, rows - NA_KEY_ROWS)
    start = pl.multiple_of(row0 * GRID_W, GRID_W)
    for hh in range(q_ref.shape[2] // HEAD_DIM):
        cols = slice(hh * HEAD_DIM, (hh + 1) * HEAD_DIM)
        q = q_ref[0, :, cols]
        s_w = _dot_nt(q, k_ref[0, pl.ds(start, nkeys), cols]) * ATTN_SCALE + bias_ref[hh, 0]
        s_c = _dot_nt(q, ck_ref[0, 0, :, cols].astype(BF16)) * ATTN_SCALE
        mx = jnp.maximum(jnp.max(s_w, axis=-1, keepdims=True), jnp.max(s_c, axis=-1, keepdims=True))
        p_w = jnp.exp2(s_w - mx)
        p_c = jnp.exp2(s_c - mx)
        den = jnp.sum(p_w, axis=-1, keepdims=True) + jnp.sum(p_c, axis=-1, keepdims=True)
        o = (_dot(p_w.astype(BF16), v_ref[0, pl.ds(start, nkeys), cols])
             + _dot(p_c.astype(BF16), cv_ref[0, 0, :, cols].astype(BF16)))
        o_ref[0, :, cols] = (o / den).astype(o_ref.dtype)


def _na_bias_table(rel_bias, rows):
    tq = NA_TILE_ROWS
    nh = rel_bias.shape[0]
    kinds = ((0, 0), (tq, 0), (rows - tq, rows - NA_KEY_ROWS))
    side = GRID_W - NA_WIN_W
    wide = jnp.pad(rel_bias.astype(F32), ((0, 0), (0, 0), (side, side)))
    cols = jnp.stack([wide[:, :, GRID_W - 1 - qc:2 * GRID_W - 1 - qc] for qc in range(GRID_W)], axis=2)
    qc, kc = np.arange(GRID_W)[:, None], np.arange(GRID_W)[None, :]
    cs = np.clip(qc - NA_WIN_W // 2, 0, GRID_W - NA_WIN_W)
    cols = jnp.where(jnp.asarray((kc >= cs) & (kc < cs + NA_WIN_W)), cols, NEG_INF)
    tiles = []
    for r0, k0 in kinds:
        per_row = []
        for ql in range(tq):
            r = r0 + ql
            rs = min(max(r - NA_WIN_H // 2, 0), rows - NA_WIN_H)
            lo, d0 = rs - k0, rs - r + NA_WIN_H - 1
            band = cols[:, d0:d0 + NA_WIN_H]
            per_row.append(jnp.pad(band, ((0, 0), (lo, NA_KEY_ROWS - NA_WIN_H - lo), (0, 0), (0, 0)),
                                   constant_values=NEG_INF))
        tiles.append(jnp.stack(per_row, axis=1))
    table = jnp.stack(tiles, axis=1).transpose(0, 1, 2, 4, 3, 5)
    return table.reshape(nh, len(kinds), tq * GRID_W, NA_KEY_ROWS * GRID_W)


def _na_attn(qkv, ck, cv, layer, bias, lay):
    b, s, _ = qkv.shape
    h = lay["heads"]
    rows = s // GRID_W
    nt = rows // NA_TILE_ROWS
    assert rows % NA_TILE_ROWS == 0 and rows >= NA_KEY_ROWS and nt >= 2
    tq, tk = NA_TILE_ROWS * GRID_W, NA_KEY_ROWS * GRID_W
    p = ck.shape[2]
    hp = next(n for n in (4, 2, 1) if h % n == 0)
    hw = hp * HEAD_DIM
    assert all(lay[n] % hp == 0 for n in ("na_q", "na_k", "na_v"))
    kind = lambda t: jnp.where(t == 0, 0, jnp.where(t == nt - 1, 2, 1))
    full = lambda col: pl.BlockSpec((1, s, hw), lambda bi, hh, t: (bi, 0, col // hp + hh))
    cache = pl.BlockSpec((1, 1, p, hw), lambda bi, hh, t: (bi, layer, 0, hh))
    return pl.pallas_call(
        functools.partial(_na_attn_kernel, rows=rows),
        name="na_attn",
        grid=(b, h // hp, nt),
        in_specs=[pl.BlockSpec((1, tq, hw), lambda bi, hh, t: (bi, t, lay["na_q"] // hp + hh)),
                  full(lay["na_k"]), full(lay["na_v"]), cache, cache,
                  pl.BlockSpec((hp, 1, tq, tk), lambda bi, hh, t: (hh, kind(t), 0, 0))],
        out_specs=pl.BlockSpec((1, tq, hw), lambda bi, hh, t: (bi, t, hh)),
        out_shape=jax.ShapeDtypeStruct((b, s, h * HEAD_DIM), BF16),
        compiler_params=_params(("parallel", "parallel", "arbitrary"), 40),
    )(qkv, qkv, qkv, ck, cv, bias)


def _rope(x, cos, sin_signed):
    lane = lax.broadcasted_iota(jnp.int32, x.shape, 1)
    partner = jnp.where(lane % 64 < 32, pltpu.roll(x, HEAD_DIM - 32, 1), pltpu.roll(x, 32, 1))
    return x * cos + partner * sin_signed


def _swa_attn_kernel(sink_ref, q_ref, k_ref, v_ref, ck_ref, cv_ref, cos_ref, sin_ref, o_ref, *, seq, rep):
    g, j = pl.program_id(1), pl.program_id(2)
    gp = k_ref.shape[2] // HEAD_DIM
    nwin = 3 * SWA_BLOCK
    start = pl.multiple_of(jnp.clip((j - 1) * SWA_BLOCK, 0, seq - nwin), SWA_BLOCK)
    q0 = pl.multiple_of(j * SWA_BLOCK, SWA_BLOCK)
    cq, sq = cos_ref[pl.ds(q0, SWA_BLOCK), :], sin_ref[pl.ds(q0, SWA_BLOCK), :]
    ckw, skw = cos_ref[pl.ds(start, nwin), :], sin_ref[pl.ds(start, nwin), :]
    qpos = q0 + lax.broadcasted_iota(jnp.int32, (SWA_BLOCK, nwin), 0)
    kpos = start + lax.broadcasted_iota(jnp.int32, (SWA_BLOCK, nwin), 1)
    mask = jnp.where(jnp.abs(qpos - kpos) <= SWA_WINDOW, 0.0, NEG_INF)
    head = lambda n: slice(n * HEAD_DIM, (n + 1) * HEAD_DIM)
    for gi in range(gp):
        q = jnp.concatenate(
            [_rope(q_ref[0, :, head(gi * rep + r)].astype(F32), cq, sq) for r in range(rep)],
            axis=0).astype(BF16)
        kw = _rope(k_ref[0, pl.ds(start, nwin), head(gi)].astype(F32), ckw, skw).astype(BF16)
        s_w = ((_dot_nt(q, kw) * ATTN_SCALE).reshape(rep, SWA_BLOCK, nwin) + mask[None]).reshape(
            rep * SWA_BLOCK, nwin)
        s_c = _dot_nt(q, ck_ref[0, 0, :, head(gi)].astype(BF16)) * ATTN_SCALE
        sink = jnp.concatenate(
            [jnp.full((SWA_BLOCK, 1), sink_ref[(g * gp + gi) * rep + r] * LOG2E, F32) for r in range(rep)], axis=0)
        mx = jnp.maximum(jnp.maximum(jnp.max(s_w, axis=-1, keepdims=True),
                                     jnp.max(s_c, axis=-1, keepdims=True)), sink)
        p_w = jnp.exp2(s_w - mx)
        p_c = jnp.exp2(s_c - mx)
        den = (jnp.sum(p_w, axis=-1, keepdims=True) + jnp.sum(p_c, axis=-1, keepdims=True)
               + jnp.exp2(sink - mx))
        o = (_dot(p_w.astype(BF16), v_ref[0, pl.ds(start, nwin), head(gi)])
             + _dot(p_c.astype(BF16), cv_ref[0, 0, :, head(gi)].astype(BF16))) / den
        for r in range(rep):
            o_ref[0, :, head(gi * rep + r)] = o[r * SWA_BLOCK:(r + 1) * SWA_BLOCK].astype(o_ref.dtype)


def _rope_tables(seq):
    t = np.arange(seq)
    pos = np.stack([t // GRID_W, t % GRID_W], axis=-1).astype(np.float32)
    nf = HEAD_DIM // 4
    inv_freq = (ROPE_BASE ** (-np.arange(nf, dtype=np.float32) / nf)).astype(np.float32)
    ang = jnp.asarray(pos)[:, :, None] * jnp.asarray(inv_freq)
    cos, sin = jnp.cos(ang), jnp.sin(ang)
    cos_t = jnp.concatenate([cos, cos], axis=-1).reshape(seq, HEAD_DIM)
    sin_t = jnp.concatenate([-sin, sin], axis=-1).reshape(seq, HEAD_DIM)
    return cos_t, sin_t


def _swa_attn(qkv, ck, cv, layer, sink, cos_t, sin_t, lay):
    b, s, _ = qkv.shape
    h, rep = lay["heads"], lay["gqa"]
    kv = h // rep
    p = ck.shape[2]
    assert s % SWA_BLOCK == 0 and s >= 3 * SWA_BLOCK
    gp = 2 if kv % 2 == 0 else 1
    kw = gp * HEAD_DIM
    qw = gp * rep * HEAD_DIM
    assert lay["sw_q"] % (gp * rep) == 0 and lay["sw_k"] % gp == 0 and lay["sw_v"] % gp == 0
    full = lambda col: pl.BlockSpec((1, s, kw), lambda bi, g, j: (bi, 0, col // gp + g))
    cache = pl.BlockSpec((1, 1, p, kw), lambda bi, g, j: (bi, layer, 0, g))
    table = pl.BlockSpec((s, HEAD_DIM), lambda bi, g, j: (0, 0))
    return pl.pallas_call(
        functools.partial(_swa_attn_kernel, seq=s, rep=rep),
        name="swa_attn",
        grid=(b, kv // gp, s // SWA_BLOCK),
        in_specs=[pl.BlockSpec(memory_space=pltpu.SMEM),
                  pl.BlockSpec((1, SWA_BLOCK, qw), lambda bi, g, j: (bi, j, lay["sw_q"] // (gp * rep) + g)),
                  full(lay["sw_k"]), full(lay["sw_v"]), cache, cache, table, table],
        out_specs=pl.BlockSpec((1, SWA_BLOCK, qw), lambda bi, g, j: (bi, j, g)),
        out_shape=jax.ShapeDtypeStruct((b, s, h * HEAD_DIM), BF16),
        compiler_params=_params(("parallel", "parallel", "arbitrary"), 40),
    )(sink, qkv, qkv, qkv, ck, cv, cos_t, sin_t)


def _dft_kernel(c_ref, s_ref, ab_ref, w_ref, o_ref, acc_ref, *, kt, norm):
    k = pl.program_id(2)
    half = ab_ref.shape[2] // 2
    part = _dot(c_ref[...], ab_ref[0, :, :half]) + _dot(s_ref[...], ab_ref[0, :, half:])

    @pl.when(k == 0)
    def _():
        acc_ref[...] = part

    @pl.when(k > 0)
    def _():
        acc_ref[...] += part

    @pl.when(k == kt - 1)
    def _():
        for g in range(w_ref.shape[0]):
            f = (acc_ref[:, g * HEAD_DIM:(g + 1) * HEAD_DIM] * norm).astype(BF16)
            o_ref[0, :, g * HEAD_DIM:(g + 1) * HEAD_DIM] = _dot(f, w_ref[g]).astype(o_ref.dtype)


def _dft_tables(seq):
    r = GRID_W
    assert seq % r == 0
    k = np.arange(seq, dtype=np.int64)
    ang_hi = jnp.asarray(((r * np.arange(seq // r)[:, None] * k) % seq) * (2.0 * np.pi / seq), F32)
    ang_lo = jnp.asarray(((np.arange(r)[:, None] * k) % seq) * (2.0 * np.pi / seq), F32)
    ch, sh = jnp.cos(ang_hi)[:, None, :], jnp.sin(ang_hi)[:, None, :]
    cl, sl = jnp.cos(ang_lo)[None, :, :], jnp.sin(ang_lo)[None, :, :]
    cos_m = (ch * cl - sh * sl).reshape(seq, seq)
    nsin_m = -(sh * cl + ch * sl).reshape(seq, seq)
    return cos_m.astype(BF16), nsin_m.astype(BF16)


def _channel_dft(groups):
    ce = np.outer(np.arange(HEAD_DIM), np.arange(HEAD_DIM)) % HEAD_DIM
    ang = 2.0 * np.pi * ce / HEAD_DIM
    eye = np.eye(groups)
    bd = np.concatenate([np.kron(eye, np.cos(ang)), np.kron(eye, np.sin(ang))], axis=1)
    return jnp.asarray(bd, dtype=BF16)


def _fourier(ab, cos_m, nsin_m, fnet_w):
    b, s, w2 = ab.shape
    w = w2 // 2
    tm, tk = _tile(s, 1024), _tile(s, 1024)
    kt = s // tk
    kern = functools.partial(_dft_kernel, kt=kt, norm=float(1.0 / np.sqrt(s * HEAD_DIM)))
    return pl.pallas_call(
        kern,
        name="fourier",
        grid=(b, s // tm, kt),
        in_specs=[pl.BlockSpec((tm, tk), lambda bi, i, k: (i, k)),
                  pl.BlockSpec((tm, tk), lambda bi, i, k: (i, k)),
                  pl.BlockSpec((1, tk, w2), lambda bi, i, k: (bi, k, 0)),
                  pl.BlockSpec(fnet_w.shape, lambda bi, i, k: (0, 0, 0))],
        out_specs=pl.BlockSpec((1, tm, w), lambda bi, i, k: (bi, i, 0)),
        out_shape=jax.ShapeDtypeStruct((b, s, w), BF16),
        scratch_shapes=[pltpu.VMEM((tm, w), F32)],
        compiler_params=_params(("parallel", "parallel", "arbitrary"), 48),
    )(cos_m, nsin_m, ab, fnet_w)


def _layout(heads, gqa):
    kv = heads // gqa
    off = np.cumsum([0, heads, heads, heads, heads, heads, kv, kv])
    return dict(heads=heads, gqa=gqa, fu=int(off[0]), na_q=int(off[1]), na_k=int(off[2]), na_v=int(off[3]),
                sw_q=int(off[4]), sw_k=int(off[5]), sw_v=int(off[6]), cols=int(off[7]) * HEAD_DIM)


def kernel(x_prompt, x_sample, c, cache_nat_k, cache_nat_v, cache_swa_k, cache_swa_v, c_ctx, ada_w, ada_b, norm_attn_pre, norm_attn_post, norm_ffn_pre, norm_ffn_post, w_in, na_bias, swa_sink, fnet_w, mix_norm, w_out, ffn_w_up, ffn_conv_w, ffn_conv_b, ffn_w_down):
    batch, seq, d = x_prompt.shape
    dec_batch, dec_seq, _ = x_sample.shape
    depth = ada_w.shape[0]
    heads = na_bias.shape[1]
    kv_heads = cache_swa_k.shape[3]
    past = cache_nat_k.shape[2]
    lay = _layout(heads, heads // kv_heads)
    width = heads * HEAD_DIM
    kvw = kv_heads * HEAD_DIM

    ncond = 1 + dec_batch
    cond = jnp.zeros((-(-ncond // 8) * 8, d), F32).at[0].set(c_ctx).at[1:ncond].set(c)
    mods = _ada(cond, ada_w, ada_b)

    cos_t, sin_t = _rope_tables(dec_seq)
    dft_ctx = _dft_tables(seq)
    dft_lat = _dft_tables(dec_seq)
    chan = _channel_dft(heads)

    fu0 = 4 * width + 2 * kvw

    xp = x_prompt.reshape(batch * seq, d)
    xs = x_sample.reshape(dec_batch * dec_seq, d)
    new_cache = [[], [], [], []]
    h_next = {True: None, False: None}
    for l in range(depth):
        w_in_l = w_in[l].astype(BF16)
        w_out_l = w_out[l].astype(BF16)
        w_up_l = ffn_w_up[l].astype(BF16)
        conv_w_l = ffn_conv_w[l].astype(F32)
        conv_b_l = ffn_conv_b[l].astype(F32)[None]
        w_down_l = ffn_w_down[l].astype(BF16)
        fnet_l = fnet_w[l].astype(BF16)
        bias_l = _na_bias_table(na_bias[l].astype(F32) * LOG2E, dec_seq // GRID_W)
        sink_ctx = jnp.concatenate([jnp.full((heads,), NEG_INF, F32), swa_sink[l].astype(F32)])

        for is_ctx in (True, False):
            x, s_len, nb = (xp, seq, batch) if is_ctx else (xs, dec_seq, dec_batch)
            rows = slice(0, 1) if is_ctx else slice(1, ncond)
            mod = lambda layer, n: mods[layer, rows, n * d:(n + 1) * d][:, None, :]
            sh_a, sc_a, g_a, sh_f, sc_f, g_f = [mod(l, n) for n in range(6)]

            h = h_next[is_ctx]
            if h is None:
                h = _modulate(x, norm_attn_pre[l], sh_a, sc_a, s_len)
            qkv = _mm(h, w_in_l, F32 if is_ctx else BF16, rotate=fu0)
            qkv3 = qkv.reshape(nb, s_len, lay["cols"])
            if is_ctx:
                o_att = _ctx_attn(qkv3, sink_ctx, lay).reshape(nb * s_len, 2 * width)
                na_part, sw_part = (o_att, 0), (o_att, 1)
                pick = lambda c0, n: qkv3[:, :, c0 * HEAD_DIM:(c0 + n) * HEAD_DIM].reshape(
                    nb, s_len, n, HEAD_DIM)
                for dst, c0, n in zip(new_cache, (lay["na_k"], lay["na_v"], lay["sw_k"], lay["sw_v"]),
                                      (heads, heads, kv_heads, kv_heads)):
                    dst.append(pick(c0, n))
            else:
                ck_na = cache_nat_k.reshape(nb, depth, past, width)
                cv_na = cache_nat_v.reshape(nb, depth, past, width)
                ck_sw = cache_swa_k.reshape(nb, depth, past, kvw)
                cv_sw = cache_swa_v.reshape(nb, depth, past, kvw)
                o_na = _na_attn(qkv3, ck_na, cv_na, l, bias_l, lay).reshape(nb * s_len, width)
                o_sw = _swa_attn(qkv3, ck_sw, cv_sw, l, swa_sink[l].astype(F32), cos_t, sin_t, lay)
                na_part, sw_part = (o_na, 0), (o_sw.reshape(nb * s_len, width), 0)
            ab = _mm(qkv, chan, BF16, tn=2 * width).reshape(nb, s_len, 2 * width)
            cos_m, nsin_m = dft_ctx if is_ctx else dft_lat
            o_fn = _fourier(ab, cos_m, nsin_m, fnet_l).reshape(nb * s_len, width)
            o_cat = _group_norm([na_part, sw_part, (o_fn, 0)], mix_norm[l])
            x, h = _mm_res(o_cat, w_out_l, x, g_a, norm_attn_post[l], s_len,
                           next_mod=(norm_ffn_pre[l], sh_f, sc_f))
            a = _ffn_up(h, w_up_l, conv_w_l, conv_b_l, s_len)
            nxt = (norm_attn_pre[l + 1], mod(l + 1, 0), mod(l + 1, 1)) if l + 1 < depth else None
            x, h_next[is_ctx] = _mm_res(a, w_down_l, x, g_f, norm_ffn_post[l], s_len, next_mod=nxt)
            if is_ctx:
                xp = x
            else:
                xs = x

    return (xp.reshape(batch, seq, d), xs.reshape(dec_batch, dec_seq, d),
            jnp.stack(new_cache[0], axis=1), jnp.stack(new_cache[1], axis=1),
            jnp.stack(new_cache[2], axis=1), jnp.stack(new_cache[3], axis=1))
```

```python
import functools

import numpy as np
import jax
import jax.numpy as jnp
from jax import lax
from jax.experimental import pallas as pl
from jax.experimental.pallas import tpu as pltpu

GRID_W = 64
HEAD_DIM = 128
NA_WIN_H = 8
NA_WIN_W = 16
NA_TILE_ROWS = 4
NA_KEY_ROWS = NA_TILE_ROWS + NA_WIN_H
SWA_WINDOW = 128
SWA_BLOCK = 128
SWA_STACK = 2
CONV_W = 3
ROPE_BASE = 10000.0
EPS = 1e-6
NEG_INF = -1e30
LOG2E = 1.4426950408889634
ATTN_SCALE = HEAD_DIM ** -0.5 * LOG2E
BF16_ROWS = 16
HALO = BF16_ROWS
MXU_WIDTH = 256
MIB = 1024 * 1024
BF16 = jnp.bfloat16
F32 = jnp.float32


def _params(sem, vmem_mib):
    return pltpu.CompilerParams(dimension_semantics=sem, vmem_limit_bytes=vmem_mib * MIB)


def _tile(n, want, align=128):
    if n <= want:
        return n
    t = want - want % align
    while t > align and n % t:
        t -= align
    assert t >= align and n % t == 0, (n, want)
    return t


def _batch_of(i, tm, seq, nb):
    return (i * tm) // seq if nb > 1 else 0


def _dot(a, b):
    return jnp.dot(a, b, preferred_element_type=F32)


def _dot_nt(a, b):
    return lax.dot_general(a, b, (((1,), (1,)), ((), ())), preferred_element_type=F32)


def _ada_kernel(c_ref, w_ref, b_ref, o_ref):
    c = c_ref[...]
    s = (c * (1.0 / (1.0 + jnp.exp(-c)))).astype(BF16)
    o_ref[0] = _dot(s, w_ref[0].astype(BF16)) + b_ref[0]


def _ada(cond, ada_w, ada_b):
    depth, d, n = ada_w.shape
    r = cond.shape[0]
    tn = _tile(n, 512)
    return pl.pallas_call(
        _ada_kernel,
        name="ada",
        grid=(depth, n // tn),
        in_specs=[pl.BlockSpec((r, d), lambda l, j: (0, 0)),
                  pl.BlockSpec((1, d, tn), lambda l, j: (l, 0, j)),
                  pl.BlockSpec((1, 1, tn), lambda l, j: (l, 0, j))],
        out_specs=pl.BlockSpec((1, r, tn), lambda l, j: (l, 0, j)),
        out_shape=jax.ShapeDtypeStruct((depth, r, n), F32),
        compiler_params=_params(("parallel", "parallel"), 40),
    )(cond, ada_w, ada_b.reshape(depth, 1, n))


def _modulate_kernel(x_ref, g_ref, sh_ref, sc_ref, o_ref):
    x = x_ref[...]
    y = x * lax.rsqrt(jnp.mean(x * x, axis=-1, keepdims=True) + EPS) * g_ref[...]
    o_ref[...] = (y * (1.0 + sc_ref[0]) + sh_ref[0]).astype(o_ref.dtype)


def _modulate(x, g, shift, scale, seq):
    m, d = x.shape
    nb = shift.shape[0]
    tm = _tile(m, 256)
    mod_spec = pl.BlockSpec((1, 1, d), lambda i: (_batch_of(i, tm, seq, nb), 0, 0))
    return pl.pallas_call(
        _modulate_kernel,
        name="modulate",
        grid=(m // tm,),
        in_specs=[pl.BlockSpec((tm, d), lambda i: (i, 0)),
                  pl.BlockSpec((1, d), lambda i: (0, 0)), mod_spec, mod_spec],
        out_specs=pl.BlockSpec((tm, d), lambda i: (i, 0)),
        out_shape=jax.ShapeDtypeStruct((m, d), BF16),
        compiler_params=_params(("parallel",), 40),
    )(x, g.reshape(1, d), shift, scale)


def _mm_kernel(a_ref, w_ref, o_ref):
    o_ref[...] = _dot(a_ref[...].astype(BF16), w_ref[...]).astype(o_ref.dtype)


def _mm(a, w, out_dtype, tm=1024, tn=512, rotate=0):
    m = a.shape[0]
    k, n = w.shape
    tm, tn = _tile(m, tm), _tile(n, tn)
    nj = n // tn
    assert rotate % tn == 0
    return pl.pallas_call(
        _mm_kernel,
        name="mm",
        grid=(m // tm, nj),
        in_specs=[pl.BlockSpec((tm, k), lambda i, j: (i, 0)),
                  pl.BlockSpec((k, tn), lambda i, j: (0, (j + rotate // tn) % nj))],
        out_specs=pl.BlockSpec((tm, tn), lambda i, j: (i, j)),
        out_shape=jax.ShapeDtypeStruct((m, n), out_dtype),
        compiler_params=_params(("parallel", "arbitrary"), 48),
    )(a, w)


def _ffn_up_kernel(prev_ref, a_ref, next_ref, wg_ref, cwg_ref, cbg_ref, *rest, tm, seq, nj, n_last):
    nch = (len(rest) - 2) // 3
    wv_refs, cwv_refs, cbv_refs = rest[:nch], rest[nch:2 * nch], rest[2 * nch:3 * nch]
    o_ref, lhs_ref = rest[3 * nch:]
    i, j = pl.program_id(0), pl.program_id(1)
    chunk = o_ref.shape[1] // nch
    rows = tm + HALO
    edges_in_halo = seq % tm == 0

    @pl.when(j == 0)
    def _():
        lhs_ref[0:tm, :] = a_ref[...]
        is_prev = lax.broadcasted_iota(jnp.int32, (HALO, 1), 0) == HALO - 1
        halo = jnp.where(is_prev, prev_ref[...].astype(F32), next_ref[...].astype(F32))
        if edges_in_halo:
            keep_prev = jnp.where((i * tm) % seq != 0, 1.0, 0.0)
            keep_next = jnp.where(((i + 1) * tm) % seq != 0, 1.0, 0.0)
            halo = halo * jnp.where(is_prev, keep_prev, keep_next)
        lhs_ref[tm:, :] = halo.astype(BF16)

    if not edges_in_halo:
        pos = (i * tm + lax.broadcasted_iota(jnp.int32, (tm, 1), 0)) % seq
        has_prev = pos != 0
        has_next = pos != seq - 1

    def conv(w, cw, cb):
        u = _dot(lhs_ref[...], w)
        up = pltpu.roll(u, 1, 0)[0:tm]
        un = pltpu.roll(u, rows - 1, 0)[0:tm]
        if not edges_in_halo:
            up = jnp.where(has_prev, up, 0.0)
            un = jnp.where(has_next, un, 0.0)
        return up * cw[0:1] + u[0:tm] * cw[1:2] + un * cw[2:3] + cb

    def body(n_valid):
        chunks = [slice(c * chunk, (c + 1) * chunk) for c in range(n_valid)]
        acts = []
        for cols in chunks:
            gate = conv(wg_ref[:, cols], cwg_ref[:, cols], cbg_ref[:, cols])
            acts.append(gate * (1.0 / (1.0 + jnp.exp(-gate))))
        for c, (cols, act) in enumerate(zip(chunks, acts)):
            val = conv(wv_refs[c][...], cwv_refs[c][...], cbv_refs[c][...])
            o_ref[:, cols] = (act * val).astype(o_ref.dtype)

    if n_last == nch:
        body(nch)
    else:
        pl.when(j < nj - 1)(functools.partial(body, nch))
        pl.when(j == nj - 1)(functools.partial(body, n_last))


def _ffn_up(h, w_up, conv_w, conv_b, seq, tm=1024, tn=512):
    m, d = h.shape
    f = w_up.shape[1] // 2
    tm, tn = _tile(m, tm), min(tn, f)
    chunk = min(MXU_WIDTH, tn)
    assert (seq % tm == 0 or tm % seq == 0) and tn % chunk == 0 and f % chunk == 0
    nj, nch = pl.cdiv(f, tn), tn // chunk
    n_last = (f - (nj - 1) * tn) // chunk
    hb = tm // HALO
    last = m // HALO - 1
    top = 2 * f // chunk - 1
    vcol = lambda c: (lambda i, j: (0, jnp.minimum(f // chunk + j * nch + c, top)))
    half = lambda rows_: [pl.BlockSpec((rows_, chunk), vcol(c)) for c in range(nch)]
    kern = functools.partial(_ffn_up_kernel, tm=tm, seq=seq, nj=nj, n_last=n_last)
    return pl.pallas_call(
        kern,
        name="ffn_up",
        grid=(m // tm, nj),
        in_specs=[pl.BlockSpec((HALO, d), lambda i, j: (jnp.maximum(i * hb - 1, 0), 0)),
                  pl.BlockSpec((tm, d), lambda i, j: (i, 0), pipeline_mode=pl.Buffered(1)),
                  pl.BlockSpec((HALO, d), lambda i, j: (jnp.minimum((i + 1) * hb, last), 0)),
                  pl.BlockSpec((d, tn), lambda i, j: (0, j)),
                  pl.BlockSpec((CONV_W, tn), lambda i, j: (0, j)),
                  pl.BlockSpec((1, tn), lambda i, j: (0, j))] + half(d) + half(CONV_W) + half(1),
        out_specs=pl.BlockSpec((tm, tn), lambda i, j: (i, j)),
        out_shape=jax.ShapeDtypeStruct((m, f), BF16),
        scratch_shapes=[pltpu.VMEM((tm + HALO, d), BF16)],
        compiler_params=_params(("parallel", "arbitrary"), 56),
    )(h, h, h, w_up, conv_w, conv_b, *([w_up] * nch), *([conv_w] * nch), *([conv_b] * nch))


def _mm_res_kernel(*refs, kt, k_last, tr, fuse_next):
    if fuse_next:
        a_ref, w_ref, x_ref, gate_ref, g_ref, gn_ref, sh_ref, sc_ref, o_ref, h_ref, acc_ref, rs_ref = refs
    else:
        a_ref, w_ref, x_ref, gate_ref, g_ref, o_ref, acc_ref, rs_ref = refs
    k = pl.program_id(1)

    def accumulate(first, last):
        kv = k_last if last else a_ref.shape[1]
        y = _dot(a_ref[:, :kv], w_ref[:kv, :])
        if not first:
            y = acc_ref[...] + y
        acc_ref[...] = y
        if last:
            rs_ref[...] = lax.rsqrt(jnp.mean(y * y, axis=-1, keepdims=True) + EPS)

    pl.when(k == 0)(functools.partial(accumulate, True, kt == 1))
    if kt > 2:
        pl.when(jnp.logical_and(k > 0, k < kt - 1))(functools.partial(accumulate, False, False))
    if kt > 1:
        pl.when(k == kt - 1)(functools.partial(accumulate, False, True))

    for r in range(acc_ref.shape[0] // tr):
        @pl.when(k == kt + r)
        def _(r=r):
            gain = gate_ref[0] * g_ref[...]
            if fuse_next:
                gain_next, shift_next = gn_ref[...] * (1.0 + sc_ref[0]), sh_ref[0]

            def strip(s, carry):
                lo = pl.multiple_of(s * BF16_ROWS, BF16_ROWS)
                dst, src = pl.ds(lo, BF16_ROWS), pl.ds(r * tr + lo, BF16_ROWS)
                x_new = x_ref[dst, :] + gain * (acc_ref[src, :] * rs_ref[src, :])
                o_ref[dst, :] = x_new
                if fuse_next:
                    inv = lax.rsqrt(jnp.mean(x_new * x_new, axis=-1, keepdims=True) + EPS)
                    h_ref[dst, :] = (x_new * inv * gain_next + shift_next).astype(h_ref.dtype)
                return carry

            lax.fori_loop(0, tr // BF16_ROWS, strip, 0, unroll=4)


def _mm_res(a, w, x, gate, g_post, seq, next_mod=None, tm=1024, tk=512, tr=256):
    m, kdim = a.shape
    d = w.shape[1]
    nb = gate.shape[0]
    tm, tk = _tile(m, tm), min(tk, kdim)
    tr = _tile(tm, tr, align=8)
    kt, nr = pl.cdiv(kdim, tk), tm // tr
    k_last = kdim - (kt - 1) * tk
    assert k_last % MXU_WIDTH == 0 or kt == 1
    fuse_next = next_mod is not None
    kk = lambda k: jnp.minimum(k, kt - 1)
    rr = lambda i, k: i * nr + jnp.clip(k - kt, 0, nr - 1)
    bb = lambda i: _batch_of(i, tm, seq, nb)
    row_spec = pl.BlockSpec((tr, d), lambda i, k: (rr(i, k), 0))
    vec_spec = pl.BlockSpec((1, d), lambda i, k: (0, 0))
    mod_spec = pl.BlockSpec((1, 1, d), lambda i, k: (bb(i), 0, 0))
    in_specs = [pl.BlockSpec((tm, tk), lambda i, k: (i, kk(k))),
                pl.BlockSpec((tk, d), lambda i, k: (kk(k), 0)),
                row_spec, mod_spec, vec_spec]
    out_specs = [row_spec]
    out_shape = [jax.ShapeDtypeStruct((m, d), F32)]
    args = [a, w, x, gate, g_post.reshape(1, d)]
    if fuse_next:
        g_next, sh_next, sc_next = next_mod
        in_specs += [vec_spec, mod_spec, mod_spec]
        out_specs.append(row_spec)
        out_shape.append(jax.ShapeDtypeStruct((m, d), BF16))
        args += [g_next.reshape(1, d), sh_next, sc_next]
    out = pl.pallas_call(
        functools.partial(_mm_res_kernel, kt=kt, k_last=k_last, tr=tr, fuse_next=fuse_next),
        name="mm_res",
        grid=(m // tm, kt + nr),
        in_specs=in_specs,
        out_specs=out_specs,
        out_shape=out_shape,
        scratch_shapes=[pltpu.VMEM((tm, d), F32), pltpu.VMEM((tm, 1), F32)],
        compiler_params=_params(("parallel", "arbitrary"), 56),
    )(*args)
    return (out[0], out[1]) if fuse_next else (out[0], None)


def _group_norm_kernel(a0_ref, a1_ref, a2_ref, g_ref, o_ref):
    w = a0_ref.shape[1]
    for n, ref in enumerate((a0_ref, a1_ref, a2_ref)):
        a = ref[...].astype(F32)
        y = a * lax.rsqrt(jnp.mean(a * a, axis=-1, keepdims=True) + EPS) * g_ref[:, n * w:(n + 1) * w]
        o_ref[:, n * w:(n + 1) * w] = y.astype(o_ref.dtype)


def _group_norm(parts, gain):
    m = parts[0][0].shape[0]
    w = gain.shape[0] // 3
    tm = _tile(m, 512)
    specs = [pl.BlockSpec((tm, w), lambda i, cb=cb: (i, cb)) for _, cb in parts]
    return pl.pallas_call(
        _group_norm_kernel,
        name="group_norm",
        grid=(m // tm,),
        in_specs=specs + [pl.BlockSpec((1, 3 * w), lambda i: (0, 0))],
        out_specs=pl.BlockSpec((tm, 3 * w), lambda i: (i, 0)),
        out_shape=jax.ShapeDtypeStruct((m, 3 * w), BF16),
        compiler_params=_params(("parallel",), 32),
    )(*[arr for arr, _ in parts], gain.reshape(1, 3 * w))


def _ctx_attn_kernel(sink_ref, q_ref, k_ref, v_ref, o_ref):
    sink = sink_ref[pl.program_id(1)] * LOG2E
    q = q_ref[0].astype(BF16)
    s = _dot_nt(q, k_ref[0].astype(BF16)) * ATTN_SCALE
    mx = jnp.maximum(jnp.max(s, axis=-1, keepdims=True), sink)
    p = jnp.exp2(s - mx)
    den = jnp.sum(p, axis=-1, keepdims=True) + jnp.exp2(sink - mx)
    o = _dot(p.astype(BF16), v_ref[0].astype(BF16))
    o_ref[0] = (o / den).astype(o_ref.dtype)


def _ctx_attn(qkv, sink, lay):
    b, s, _ = qkv.shape
    h, rep = lay["heads"], lay["gqa"]
    grp = lambda hh: hh >= h
    qcol = lambda hh: jnp.where(grp(hh), lay["sw_q"] + hh - h, lay["na_q"] + hh)
    kcol = lambda hh: jnp.where(grp(hh), lay["sw_k"] + (hh - h) // rep, lay["na_k"] + hh)
    vcol = lambda hh: jnp.where(grp(hh), lay["sw_v"] + (hh - h) // rep, lay["na_v"] + hh)
    blk = lambda col: pl.BlockSpec((1, s, HEAD_DIM), lambda bi, hh: (bi, 0, col(hh)))
    return pl.pallas_call(
        _ctx_attn_kernel,
        name="ctx_attn",
        grid=(b, 2 * h),
        in_specs=[pl.BlockSpec(memory_space=pltpu.SMEM), blk(qcol), blk(kcol), blk(vcol)],
        out_specs=pl.BlockSpec((1, s, HEAD_DIM), lambda bi, hh: (bi, 0, hh)),
        out_shape=jax.ShapeDtypeStruct((b, s, 2 * h * HEAD_DIM), BF16),
        compiler_params=_params(("parallel", "parallel"), 32),
    )(sink, qkv, qkv, qkv)


def _na_attn_kernel(q_ref, k_ref, v_ref, ck_ref, cv_ref, bias_ref, o_ref, *, rows):
    t = pl.program_id(2)
    nkeys = NA_KEY_ROWS * GRID_W
    row0 = jnp.clip(NA_TILE_ROWS * t - NA_WIN_H // 2, 0, rows - NA_KEY_ROWS)
    start = pl.multiple_of(row0 * GRID_W, GRID_W)
    for hh in range(q_ref.shape[2] // HEAD_DIM):
        cols = slice(hh * HEAD_DIM, (hh + 1) * HEAD_DIM)
        q = q_ref[0, :, cols]
        s_w = _dot_nt(q, k_ref[0, pl.ds(start, nkeys), cols]) * ATTN_SCALE + bias_ref[hh, 0]
        s_c = _dot_nt(q, ck_ref[0, 0, :, cols].astype(BF16)) * ATTN_SCALE
        mx = jnp.maximum(jnp.max(s_w, axis=-1, keepdims=True), jnp.max(s_c, axis=-1, keepdims=True))
        p_w = jnp.exp2(s_w - mx)
        p_c = jnp.exp2(s_c - mx)
        den = jnp.sum(p_w, axis=-1, keepdims=True) + jnp.sum(p_c, axis=-1, keepdims=True)
        o = (_dot(p_w.astype(BF16), v_ref[0, pl.ds(start, nkeys), cols])
             + _dot(p_c.astype(BF16), cv_ref[0, 0, :, cols].astype(BF16)))
        o_ref[0, :, cols] = (o / den).astype(o_ref.dtype)


def _na_bias_table(rel_bias, rows):
    tq = NA_TILE_ROWS
    nh = rel_bias.shape[0]
    kinds = ((0, 0), (tq, 0), (rows - tq, rows - NA_KEY_ROWS))
    side = GRID_W - NA_WIN_W
    wide = jnp.pad(rel_bias.astype(F32), ((0, 0), (0, 0), (side, side)))
    cols = jnp.stack([wide[:, :, GRID_W - 1 - qc:2 * GRID_W - 1 - qc] for qc in range(GRID_W)], axis=2)
    qc, kc = np.arange(GRID_W)[:, None], np.arange(GRID_W)[None, :]
    cs = np.clip(qc - NA_WIN_W // 2, 0, GRID_W - NA_WIN_W)
    cols = jnp.where(jnp.asarray((kc >= cs) & (kc < cs + NA_WIN_W)), cols, NEG_INF)
    tiles = []
    for r0, k0 in kinds:
        per_row = []
        for ql in range(tq):
            r = r0 + ql
            rs = min(max(r - NA_WIN_H // 2, 0), rows - NA_WIN_H)
            lo, d0 = rs - k0, rs - r + NA_WIN_H - 1
            band = cols[:, d0:d0 + NA_WIN_H]
            per_row.append(jnp.pad(band, ((0, 0), (lo, NA_KEY_ROWS - NA_WIN_H - lo), (0, 0), (0, 0)),
                                   constant_values=NEG_INF))
        tiles.append(jnp.stack(per_row, axis=1))
    table = jnp.stack(tiles, axis=1).transpose(0, 1, 2, 4, 3, 5)
    return table.reshape(nh, len(kinds), tq * GRID_W, NA_KEY_ROWS * GRID_W)


def _na_attn(qkv, ck, cv, layer, bias, lay):
    b, s, _ = qkv.shape
    h = lay["heads"]
    rows = s // GRID_W
    nt = rows // NA_TILE_ROWS
    assert rows % NA_TILE_ROWS == 0 and rows >= NA_KEY_ROWS and nt >= 2
    tq, tk = NA_TILE_ROWS * GRID_W, NA_KEY_ROWS * GRID_W
    p = ck.shape[2]
    hp = next(n for n in (4, 2, 1) if h % n == 0)
    hw = hp * HEAD_DIM
    assert all(lay[n] % hp == 0 for n in ("na_q", "na_k", "na_v"))
    kind = lambda t: jnp.where(t == 0, 0, jnp.where(t == nt - 1, 2, 1))
    full = lambda col: pl.BlockSpec((1, s, hw), lambda bi, hh, t: (bi, 0, col // hp + hh))
    cache = pl.BlockSpec((1, 1, p, hw), lambda bi, hh, t: (bi, layer, 0, hh))
    return pl.pallas_call(
        functools.partial(_na_attn_kernel, rows=rows),
        name="na_attn",
        grid=(b, h // hp, nt),
        in_specs=[pl.BlockSpec((1, tq, hw), lambda bi, hh, t: (bi, t, lay["na_q"] // hp + hh)),
                  full(lay["na_k"]), full(lay["na_v"]), cache, cache,
                  pl.BlockSpec((hp, 1, tq, tk), lambda bi, hh, t: (hh, kind(t), 0, 0))],
        out_specs=pl.BlockSpec((1, tq, hw), lambda bi, hh, t: (bi, t, hh)),
        out_shape=jax.ShapeDtypeStruct((b, s, h * HEAD_DIM), BF16),
        compiler_params=_params(("parallel", "parallel", "arbitrary"), 40),
    )(qkv, qkv, qkv, ck, cv, bias)


def _rope(x, cos, sin_signed):
    lane = lax.broadcasted_iota(jnp.int32, x.shape, 1)
    partner = jnp.where(lane % 64 < 32, pltpu.roll(x, HEAD_DIM - 32, 1), pltpu.roll(x, 32, 1))
    return x * cos + partner * sin_signed


def _swa_attn_kernel(sink_ref, q_ref, k_ref, v_ref, ck_ref, cv_ref, cos_ref, sin_ref, o_ref, *, seq, rep):
    g, j = pl.program_id(1), pl.program_id(2)
    gp = k_ref.shape[2] // HEAD_DIM
    nwin = 3 * SWA_BLOCK
    start = pl.multiple_of(jnp.clip((j - 1) * SWA_BLOCK, 0, seq - nwin), SWA_BLOCK)
    q0 = pl.multiple_of(j * SWA_BLOCK, SWA_BLOCK)
    cq, sq = cos_ref[pl.ds(q0, SWA_BLOCK), :], sin_ref[pl.ds(q0, SWA_BLOCK), :]
    ckw, skw = cos_ref[pl.ds(start, nwin), :], sin_ref[pl.ds(start, nwin), :]
    qpos = q0 + lax.broadcasted_iota(jnp.int32, (SWA_BLOCK, nwin), 0)
    kpos = start + lax.broadcasted_iota(jnp.int32, (SWA_BLOCK, nwin), 1)
    mask = jnp.where(jnp.abs(qpos - kpos) <= SWA_WINDOW, 0.0, NEG_INF)
    head = lambda n: slice(n * HEAD_DIM, (n + 1) * HEAD_DIM)
    nst = min(SWA_STACK, rep)
    for gi in range(gp):
        kw = _rope(k_ref[0, pl.ds(start, nwin), head(gi)].astype(F32), ckw, skw).astype(BF16)
        vw = v_ref[0, pl.ds(start, nwin), head(gi)]
        ck, cv = ck_ref[0, 0, :, head(gi)].astype(BF16), cv_ref[0, 0, :, head(gi)].astype(BF16)
        for h0 in range(gi * rep, (gi + 1) * rep, nst):
            q = jnp.concatenate(
                [_rope(q_ref[0, :, head(h0 + r)].astype(F32), cq, sq) for r in range(nst)],
                axis=0).astype(BF16)
            s_w = ((_dot_nt(q, kw) * ATTN_SCALE).reshape(nst, SWA_BLOCK, nwin) + mask[None]).reshape(
                nst * SWA_BLOCK, nwin)
            s_c = _dot_nt(q, ck) * ATTN_SCALE
            sink = jnp.concatenate(
                [jnp.full((SWA_BLOCK, 1), sink_ref[g * gp * rep + h0 + r] * LOG2E, F32) for r in range(nst)],
                axis=0)
            mx = jnp.maximum(jnp.maximum(jnp.max(s_w, axis=-1, keepdims=True),
                                         jnp.max(s_c, axis=-1, keepdims=True)), sink)
            p_w = jnp.exp2(s_w - mx)
            p_c = jnp.exp2(s_c - mx)
            den = (jnp.sum(p_w, axis=-1, keepdims=True) + jnp.sum(p_c, axis=-1, keepdims=True)
                   + jnp.exp2(sink - mx))
            o = (_dot(p_w.astype(BF16), vw) + _dot(p_c.astype(BF16), cv)) / den
            for r in range(nst):
                o_ref[0, :, head(h0 + r)] = o[r * SWA_BLOCK:(r + 1) * SWA_BLOCK].astype(o_ref.dtype)


def _rope_tables(seq):
    t = np.arange(seq)
    pos = np.stack([t // GRID_W, t % GRID_W], axis=-1).astype(np.float32)
    nf = HEAD_DIM // 4
    inv_freq = (ROPE_BASE ** (-np.arange(nf, dtype=np.float32) / nf)).astype(np.float32)
    ang = jnp.asarray(pos)[:, :, None] * jnp.asarray(inv_freq)
    cos, sin = jnp.cos(ang), jnp.sin(ang)
    cos_t = jnp.concatenate([cos, cos], axis=-1).reshape(seq, HEAD_DIM)
    sin_t = jnp.concatenate([-sin, sin], axis=-1).reshape(seq, HEAD_DIM)
    return cos_t, sin_t


def _swa_attn(qkv, ck, cv, layer, sink, cos_t, sin_t, lay):
    b, s, _ = qkv.shape
    h, rep = lay["heads"], lay["gqa"]
    kv = h // rep
    p = ck.shape[2]
    assert s % SWA_BLOCK == 0 and s >= 3 * SWA_BLOCK
    gp = 2 if kv % 2 == 0 else 1
    kw = gp * HEAD_DIM
    qw = gp * rep * HEAD_DIM
    assert lay["sw_q"] % (gp * rep) == 0 and lay["sw_k"] % gp == 0 and lay["sw_v"] % gp == 0
    full = lambda col: pl.BlockSpec((1, s, kw), lambda bi, g, j: (bi, 0, col // gp + g))
    cache = pl.BlockSpec((1, 1, p, kw), lambda bi, g, j: (bi, layer, 0, g))
    table = pl.BlockSpec((s, HEAD_DIM), lambda bi, g, j: (0, 0))
    return pl.pallas_call(
        functools.partial(_swa_attn_kernel, seq=s, rep=rep),
        name="swa_attn",
        grid=(b, kv // gp, s // SWA_BLOCK),
        in_specs=[pl.BlockSpec(memory_space=pltpu.SMEM),
                  pl.BlockSpec((1, SWA_BLOCK, qw), lambda bi, g, j: (bi, j, lay["sw_q"] // (gp * rep) + g)),
                  full(lay["sw_k"]), full(lay["sw_v"]), cache, cache, table, table],
        out_specs=pl.BlockSpec((1, SWA_BLOCK, qw), lambda bi, g, j: (bi, j, g)),
        out_shape=jax.ShapeDtypeStruct((b, s, h * HEAD_DIM), BF16),
        compiler_params=_params(("parallel", "parallel", "arbitrary"), 40),
    )(sink, qkv, qkv, qkv, ck, cv, cos_t, sin_t)


def _dft_kernel(c_ref, s_ref, ab_ref, w_ref, o_ref, acc_ref, *, kt, norm):
    k = pl.program_id(2)
    half = ab_ref.shape[2] // 2
    part = _dot(c_ref[...], ab_ref[0, :, :half]) + _dot(s_ref[...], ab_ref[0, :, half:])

    @pl.when(k == 0)
    def _():
        acc_ref[...] = part

    @pl.when(k > 0)
    def _():
        acc_ref[...] += part

    @pl.when(k == kt - 1)
    def _():
        for g in range(w_ref.shape[0]):
            f = (acc_ref[:, g * HEAD_DIM:(g + 1) * HEAD_DIM] * norm).astype(BF16)
            o_ref[0, :, g * HEAD_DIM:(g + 1) * HEAD_DIM] = _dot(f, w_ref[g]).astype(o_ref.dtype)


def _dft_tables(seq):
    r = GRID_W
    assert seq % r == 0
    k = np.arange(seq, dtype=np.int64)
    ang_hi = jnp.asarray(((r * np.arange(seq // r)[:, None] * k) % seq) * (2.0 * np.pi / seq), F32)
    ang_lo = jnp.asarray(((np.arange(r)[:, None] * k) % seq) * (2.0 * np.pi / seq), F32)
    ch, sh = jnp.cos(ang_hi)[:, None, :], jnp.sin(ang_hi)[:, None, :]
    cl, sl = jnp.cos(ang_lo)[None, :, :], jnp.sin(ang_lo)[None, :, :]
    cos_m = (ch * cl - sh * sl).reshape(seq, seq)
    nsin_m = -(sh * cl + ch * sl).reshape(seq, seq)
    return cos_m.astype(BF16), nsin_m.astype(BF16)


def _channel_dft(groups):
    ce = np.outer(np.arange(HEAD_DIM), np.arange(HEAD_DIM)) % HEAD_DIM
    ang = 2.0 * np.pi * ce / HEAD_DIM
    eye = np.eye(groups)
    bd = np.concatenate([np.kron(eye, np.cos(ang)), np.kron(eye, np.sin(ang))], axis=1)
    return jnp.asarray(bd, dtype=BF16)


def _fourier(ab, cos_m, nsin_m, fnet_w):
    b, s, w2 = ab.shape
    w = w2 // 2
    tm, tk = _tile(s, 1024), _tile(s, 1024)
    kt = s // tk
    kern = functools.partial(_dft_kernel, kt=kt, norm=float(1.0 / np.sqrt(s * HEAD_DIM)))
    return pl.pallas_call(
        kern,
        name="fourier",
        grid=(b, s // tm, kt),
        in_specs=[pl.BlockSpec((tm, tk), lambda bi, i, k: (i, k)),
                  pl.BlockSpec((tm, tk), lambda bi, i, k: (i, k)),
                  pl.BlockSpec((1, tk, w2), lambda bi, i, k: (bi, k, 0)),
                  pl.BlockSpec(fnet_w.shape, lambda bi, i, k: (0, 0, 0))],
        out_specs=pl.BlockSpec((1, tm, w), lambda bi, i, k: (bi, i, 0)),
        out_shape=jax.ShapeDtypeStruct((b, s, w), BF16),
        scratch_shapes=[pltpu.VMEM((tm, w), F32)],
        compiler_params=_params(("parallel", "parallel", "arbitrary"), 48),
    )(cos_m, nsin_m, ab, fnet_w)


def _layout(heads, gqa):
    kv = heads // gqa
    off = np.cumsum([0, heads, heads, heads, heads, heads, kv, kv])
    return dict(heads=heads, gqa=gqa, fu=int(off[0]), na_q=int(off[1]), na_k=int(off[2]), na_v=int(off[3]),
                sw_q=int(off[4]), sw_k=int(off[5]), sw_v=int(off[6]), cols=int(off[7]) * HEAD_DIM)


def kernel(x_prompt, x_sample, c, cache_nat_k, cache_nat_v, cache_swa_k, cache_swa_v, c_ctx, ada_w, ada_b, norm_attn_pre, norm_attn_post, norm_ffn_pre, norm_ffn_post, w_in, na_bias, swa_sink, fnet_w, mix_norm, w_out, ffn_w_up, ffn_conv_w, ffn_conv_b, ffn_w_down):
    batch, seq, d = x_prompt.shape
    dec_batch, dec_seq, _ = x_sample.shape
    depth = ada_w.shape[0]
    heads = na_bias.shape[1]
    kv_heads = cache_swa_k.shape[3]
    past = cache_nat_k.shape[2]
    lay = _layout(heads, heads // kv_heads)
    width = heads * HEAD_DIM
    kvw = kv_heads * HEAD_DIM

    ncond = 1 + dec_batch
    cond = jnp.zeros((-(-ncond // 8) * 8, d), F32).at[0].set(c_ctx).at[1:ncond].set(c)
    mods = _ada(cond, ada_w, ada_b)

    cos_t, sin_t = _rope_tables(dec_seq)
    dft_ctx = _dft_tables(seq)
    dft_lat = _dft_tables(dec_seq)
    chan = _channel_dft(heads)

    fu0 = 4 * width + 2 * kvw

    xp = x_prompt.reshape(batch * seq, d)
    xs = x_sample.reshape(dec_batch * dec_seq, d)
    new_cache = [[], [], [], []]
    h_next = {True: None, False: None}
    for l in range(depth):
        w_in_l = w_in[l].astype(BF16)
        w_out_l = w_out[l].astype(BF16)
        w_up_l = ffn_w_up[l].astype(BF16)
        conv_w_l = ffn_conv_w[l].astype(F32)
        conv_b_l = ffn_conv_b[l].astype(F32)[None]
        w_down_l = ffn_w_down[l].astype(BF16)
        fnet_l = fnet_w[l].astype(BF16)
        bias_l = _na_bias_table(na_bias[l].astype(F32) * LOG2E, dec_seq // GRID_W)
        sink_ctx = jnp.concatenate([jnp.full((heads,), NEG_INF, F32), swa_sink[l].astype(F32)])

        for is_ctx in (True, False):
            x, s_len, nb = (xp, seq, batch) if is_ctx else (xs, dec_seq, dec_batch)
            rows = slice(0, 1) if is_ctx else slice(1, ncond)
            mod = lambda layer, n: mods[layer, rows, n * d:(n + 1) * d][:, None, :]
            sh_a, sc_a, g_a, sh_f, sc_f, g_f = [mod(l, n) for n in range(6)]

            h = h_next[is_ctx]
            if h is None:
                h = _modulate(x, norm_attn_pre[l], sh_a, sc_a, s_len)
            qkv = _mm(h, w_in_l, F32 if is_ctx else BF16, rotate=fu0)
            qkv3 = qkv.reshape(nb, s_len, lay["cols"])
            if is_ctx:
                o_att = _ctx_attn(qkv3, sink_ctx, lay).reshape(nb * s_len, 2 * width)
                na_part, sw_part = (o_att, 0), (o_att, 1)
                pick = lambda c0, n: qkv3[:, :, c0 * HEAD_DIM:(c0 + n) * HEAD_DIM].reshape(
                    nb, s_len, n, HEAD_DIM)
                for dst, c0, n in zip(new_cache, (lay["na_k"], lay["na_v"], lay["sw_k"], lay["sw_v"]),
                                      (heads, heads, kv_heads, kv_heads)):
                    dst.append(pick(c0, n))
            else:
                ck_na = cache_nat_k.reshape(nb, depth, past, width)
                cv_na = cache_nat_v.reshape(nb, depth, past, width)
                ck_sw = cache_swa_k.reshape(nb, depth, past, kvw)
                cv_sw = cache_swa_v.reshape(nb, depth, past, kvw)
                o_na = _na_attn(qkv3, ck_na, cv_na, l, bias_l, lay).reshape(nb * s_len, width)
                o_sw = _swa_attn(qkv3, ck_sw, cv_sw, l, swa_sink[l].astype(F32), cos_t, sin_t, lay)
                na_part, sw_part = (o_na, 0), (o_sw.reshape(nb * s_len, width), 0)
            ab = _mm(qkv, chan, BF16, tn=2 * width).reshape(nb, s_len, 2 * width)
            cos_m, nsin_m = dft_ctx if is_ctx else dft_lat
            o_fn = _fourier(ab, cos_m, nsin_m, fnet_l).reshape(nb * s_len, width)
            o_cat = _group_norm([na_part, sw_part, (o_fn, 0)], mix_norm[l])
            x, h = _mm_res(o_cat, w_out_l, x, g_a, norm_attn_post[l], s_len,
                           next_mod=(norm_ffn_pre[l], sh_f, sc_f))
            a = _ffn_up(h, w_up_l, conv_w_l, conv_b_l, s_len)
            nxt = (norm_attn_pre[l + 1], mod(l + 1, 0), mod(l + 1, 1)) if l + 1 < depth else None
            x, h_next[is_ctx] = _mm_res(a, w_down_l, x, g_f, norm_ffn_post[l], s_len, next_mod=nxt)
            if is_ctx:
                xp = x
            else:
                xs = x

    return (xp.reshape(batch, seq, d), xs.reshape(dec_batch, dec_seq, d),
            jnp.stack(new_cache[0], axis=1), jnp.stack(new_cache[1], axis=1),
            jnp.stack(new_cache[2], axis=1), jnp.stack(new_cache[3], axis=1))
```
